```python
import math
import jax, jax.numpy as jnp
from jax import lax
import numpy as np

D_MODEL = 4096
BATCH = 1
SEQ = 8192
DEPTH = 2

N_MIXERS = 2
N_HEADS = 32
HEAD_DIM = D_MODEL // N_HEADS
ATT_WIDTH = N_HEADS * HEAD_DIM
CONV_WIDTH = D_MODEL
CONV_GROUPS = 32
CONV_K = 3
Q_BLOCK = 128
RMS_EPS = 1e-6

kernel_name = "hybrid_stickbreak_shortconv_trunk"


def rms_norm(x, g):
    xf = x.astype(jnp.float32)
    y = xf * lax.rsqrt(jnp.mean(xf * xf, axis=-1, keepdims=True) + RMS_EPS)
    return (y * g.astype(jnp.float32)).astype(x.dtype)


def stick_breaking_attention(q, k, v):
    b, h, s, dh = q.shape
    nb = s // Q_BLOCK
    scale = 1.0 / math.sqrt(dh)
    q_blocks = q.reshape(b, h, nb, Q_BLOCK, dh).transpose(2, 0, 1, 3, 4)
    s_pos = jnp.arange(s)

    def one_block(args):
        blk, qb = args
        t_pos = blk * Q_BLOCK + jnp.arange(Q_BLOCK)
        causal = s_pos[None, :] < t_pos[:, None]
        z = jnp.einsum('bhqd,bhkd->bhqk', qb, k,
                       preferred_element_type=jnp.float32) * scale
        log_fail = jnp.where(causal, jax.nn.log_sigmoid(-z), 0.0)
        suffix = lax.cumsum(log_fail, axis=3, reverse=True) - log_fail
        log_a = jax.nn.log_sigmoid(z) + suffix
        a = jnp.where(causal, jnp.exp(jnp.where(causal, log_a, 0.0)), 0.0)
        return jnp.einsum('bhqk,bhkd->bhqd', a.astype(v.dtype), v)

    out = lax.map(one_block, (jnp.arange(nb), q_blocks))
    return out.transpose(1, 2, 0, 3, 4).reshape(b, h, s, dh)


def attn_branch(xn, w_in, w_out):
    b, s, _ = xn.shape
    proj = xn @ w_in
    q, k, v, g = jnp.split(proj, 4, axis=-1)
    to_heads = lambda t: t.reshape(b, s, N_HEADS, HEAD_DIM).transpose(0, 2, 1, 3)
    o = stick_breaking_attention(to_heads(q), to_heads(k), to_heads(v))
    o = o.transpose(0, 2, 1, 3).reshape(b, s, ATT_WIDTH)
    return (o * jax.nn.silu(g)) @ w_out


def causal_depthwise_conv(u, w):
    c = u.shape[-1]
    return lax.conv_general_dilated(
        u, w.astype(u.dtype)[:, None, :], window_strides=(1,),
        padding=((CONV_K - 1, 0),), dimension_numbers=('NWC', 'WIO', 'NWC'),
        feature_group_count=c)


def conv_branch(xn, w_in, conv_w, w_out):
    proj = xn @ w_in
    gb, gc, u, g = jnp.split(proj, 4, axis=-1)
    y = gb * causal_depthwise_conv(gc * u, conv_w)
    return (y * jax.nn.silu(g)) @ w_out


def setup_inputs(seed: int = 0) -> dict:
    key = jax.random.key(seed)
    ks = jax.random.split(key, 10)
    f32 = jnp.float32
    nrm = lambda k, shape, fan_in: jax.random.normal(k, shape, f32) * (fan_in ** -0.5)
    gain = lambda k: 1.0 + 0.02 * jax.random.normal(k, (D_MODEL,), f32)
    return {
        "x": jax.random.normal(ks[0], (BATCH, SEQ, D_MODEL), f32),
        "norm_attn": gain(ks[1]),
        "w_in_attn": nrm(ks[2], (D_MODEL, 4 * ATT_WIDTH), D_MODEL),
        "w_out_attn": nrm(ks[3], (ATT_WIDTH, D_MODEL), ATT_WIDTH),
        "norm_conv": gain(ks[4]),
        "w_in_conv": nrm(ks[5], (D_MODEL, 4 * CONV_WIDTH), D_MODEL),
        "conv_w": nrm(ks[6], (CONV_K, CONV_WIDTH), CONV_K),
        "w_out_conv": nrm(ks[7], (CONV_WIDTH, D_MODEL), CONV_WIDTH),
        "final_norm": gain(ks[8]),
    }


def reference(x, norm_attn, w_in_attn, w_out_attn, norm_conv, w_in_conv, conv_w,
              w_out_conv, final_norm):
    h = x
    for i in range(DEPTH):
        if i % N_MIXERS == 0:
            h = h + attn_branch(rms_norm(h, norm_attn), w_in_attn, w_out_attn)
        else:
            h = h + conv_branch(rms_norm(h, norm_conv), w_in_conv, conv_w, w_out_conv)
    return rms_norm(h, final_norm)
```

```python
import functools
import math

import jax
import jax.numpy as jnp
from jax import lax
from jax.experimental import pallas as pl
from jax.experimental.pallas import tpu as pltpu

F32 = jnp.float32
BF16 = jnp.bfloat16

RMS_EPS = 1e-6
N_HEADS = 32
HEAD_DIM = 128
CONV_K = 3

LANES = 128
SUBLANES = 8
VMEM_BYTES_V7X = 64 * 1024 * 1024

QB = 128
KB = 128
LOG_ZERO_F32 = -105.0

NORM_ROWS = 16


def _vmem_limit(block_bytes, scratch_bytes, temp_bytes):
    need = 2 * block_bytes + scratch_bytes + temp_bytes + (4 << 20)
    return int(min(need, VMEM_BYTES_V7X - (4 << 20)))


def _silu(g):
    return g * (1.0 / (1.0 + jnp.exp(-g)))


def _rmsnorm_rows(x_ref, gain_ref, dst_ref, rows):
    def body(r, _):
        sl = pl.ds(pl.multiple_of(r * NORM_ROWS, NORM_ROWS), NORM_ROWS)
        xf = x_ref[sl, :]
        ms = jnp.mean(xf * xf, axis=-1, keepdims=True)
        dst_ref[sl, :] = (xf * lax.rsqrt(ms + RMS_EPS) * gain_ref[...]).astype(dst_ref.dtype)
        return 0
    lax.fori_loop(0, rows // NORM_ROWS, body, 0)


def _proj4(xn_ref, w_refs):
    xn = xn_ref[...]
    return [jnp.dot(xn, w[...], preferred_element_type=F32) for w in w_refs]


def _attn_inproj_kernel(x_ref, gain_ref, wq, wk, wv, wg, q_ref, k_ref, v_ref, sg_ref, xn_ref):
    tm = x_ref.shape[0]

    @pl.when(pl.program_id(1) == 0)
    def _():
        _rmsnorm_rows(x_ref, gain_ref, xn_ref, tm)

    q, k, v, g = _proj4(xn_ref, (wq, wk, wv, wg))
    sg = _silu(g)
    for hh in range(q_ref.shape[0]):
        cols = slice(hh * HEAD_DIM, (hh + 1) * HEAD_DIM)
        q_ref[hh] = q[:, cols].astype(BF16)
        k_ref[hh] = k[:, cols].astype(BF16)
        v_ref[hh] = v[:, cols].astype(BF16)
        sg_ref[hh] = sg[:, cols]


def _conv_inproj_kernel(x_ref, gain_ref, wb, wc, wu, wg, cw_ref, y_ref, xn_ref, halo_ref):
    tm = x_ref.shape[0]
    m = pl.program_id(0)
    c = pl.program_id(1)

    @pl.when(c == 0)
    def _():
        _rmsnorm_rows(x_ref, gain_ref, xn_ref, tm)

    gb, gc, u, g = _proj4(xn_ref, (wb, wc, wu, wg))
    cu = gc * u

    halo = jnp.where(m > 0, halo_ref[c], 0.0)
    prev1 = halo[SUBLANES - 1:SUBLANES, :]
    prev2 = halo[SUBLANES - 2:SUBLANES - 1, :]
    row = lax.broadcasted_iota(jnp.int32, cu.shape, 0)
    cu1 = jnp.where(row == 0, prev1, pltpu.roll(cu, 1, 0))
    cu2 = jnp.where(row == 0, prev2, jnp.where(row == 1, prev1, pltpu.roll(cu, 2, 0)))
    halo_ref[c] = cu[tm - SUBLANES:, :]

    cw = cw_ref[...]
    conv = cw[0:1, :] * cu2 + cw[1:2, :] * cu1 + cw[2:3, :] * cu
    y_ref[...] = ((gb * conv) * _silu(g)).astype(y_ref.dtype)


def _inproj_common(x, gain, w_bf16, tm, tn):
    s, d = x.shape
    width = w_bf16.shape[1] // 4
    nblk = width // tn
    x_spec = pl.BlockSpec((tm, d), lambda m, c: (m, 0))
    gain_spec = pl.BlockSpec((1, d), lambda m, c: (0, 0))
    w_specs = [pl.BlockSpec((d, tn), functools.partial(lambda m, c, j: (0, j * nblk + c), j=j))
               for j in range(4)]
    block_bytes = tm * d * 4 + d * 4 + 4 * d * tn * 2
    return s, d, width, nblk, x_spec, gain_spec, w_specs, block_bytes


def _attn_inproj(x, gain, w_bf16, tm=512, tn=256):
    s, d, width, nblk, x_spec, gain_spec, w_specs, block_bytes = _inproj_common(x, gain, w_bf16, tm, tn)
    hpt = tn // HEAD_DIM
    head_spec = pl.BlockSpec((hpt, tm, HEAD_DIM), lambda m, c: (c, m, 0))
    out_bytes = hpt * tm * HEAD_DIM * (3 * 2 + 4)
    return pl.pallas_call(
        _attn_inproj_kernel,
        grid=(s // tm, nblk),
        in_specs=[x_spec, gain_spec] + w_specs,
        out_specs=[head_spec] * 4,
        out_shape=[jax.ShapeDtypeStruct((N_HEADS, s, HEAD_DIM), BF16)] * 3
        + [jax.ShapeDtypeStruct((N_HEADS, s, HEAD_DIM), F32)],
        scratch_shapes=[pltpu.VMEM((tm, d), BF16)],
        compiler_params=pltpu.CompilerParams(
            dimension_semantics=("arbitrary", "arbitrary"),
            vmem_limit_bytes=_vmem_limit(block_bytes + out_bytes, tm * d * 2, 8 * tm * tn * 4)),
        name="attn_inproj",
    )(x, gain, w_bf16, w_bf16, w_bf16, w_bf16)


def _conv_inproj(x, gain, w_bf16, conv_w, tm=512, tn=256):
    s, d, width, nblk, x_spec, gain_spec, w_specs, block_bytes = _inproj_common(x, gain, w_bf16, tm, tn)
    cw_spec = pl.BlockSpec((CONV_K, tn), lambda m, c: (0, c))
    y_spec = pl.BlockSpec((tm, tn), lambda m, c: (m, c))
    return pl.pallas_call(
        _conv_inproj_kernel,
        grid=(s // tm, nblk),
        in_specs=[x_spec, gain_spec] + w_specs + [cw_spec],
        out_specs=y_spec,
        out_shape=jax.ShapeDtypeStruct((s, width), BF16),
        scratch_shapes=[pltpu.VMEM((tm, d), BF16), pltpu.VMEM((nblk, SUBLANES, tn), F32)],
        compiler_params=pltpu.CompilerParams(
            dimension_semantics=("arbitrary", "arbitrary"),
            vmem_limit_bytes=_vmem_limit(block_bytes + tm * tn * 2, tm * d * 2, 10 * tm * tn * 4)),
        name="conv_inproj",
    )(x, gain, w_bf16, w_bf16, w_bf16, w_bf16, conv_w)


def _softplus(z):
    return jnp.maximum(z, 0.0) + jnp.log1p(jnp.exp(-jnp.abs(z)))


def _suffix_sums(lf, mm):
    hi = lf.astype(BF16)
    lo = (lf - hi.astype(F32)).astype(BF16)
    return jnp.dot(jnp.concatenate([hi, lo], axis=1), mm, preferred_element_type=F32)


def _attn_kernel(q_ref, k_ref, v_ref, sg_ref, o_ref, mm_ref):
    s = q_ref.shape[0]
    scale = 1.0 / math.sqrt(HEAD_DIM)

    r = lax.broadcasted_iota(jnp.int32, (2 * KB, 2 * KB), 0) & (KB - 1)
    cc = lax.broadcasted_iota(jnp.int32, (2 * KB, 2 * KB), 1)
    mm_ref[...] = jnp.where((cc >= KB) | (r >= cc), 1.0, 0.0).astype(BF16)

    nt = (((1,), (1,)), ((), ()))

    def qblock(i, _):
        r0 = pl.multiple_of(i * QB, QB)
        q = q_ref[pl.ds(r0, QB), :]
        mm = mm_ref[...]

        row = lax.broadcasted_iota(jnp.int32, (QB, KB), 0)
        col = lax.broadcasted_iota(jnp.int32, (QB, KB), 1)
        tri = col < row
        z = lax.dot_general(q, k_ref[pl.ds(r0, KB), :], nt, preferred_element_type=F32) * scale
        lf = jnp.where(tri, -_softplus(z), 0.0)
        p = _suffix_sums(lf, mm)
        a = jnp.where(tri, jnp.exp(z + p[:, :KB]), 0.0)
        acc = jnp.dot(a.astype(BF16), v_ref[pl.ds(r0, KB), :], preferred_element_type=F32)
        carry = p[:, KB:]

        def cond(st):
            kb, _, _, mx = st
            return jnp.logical_and(kb >= 0, mx > LOG_ZERO_F32)

        def body(st):
            kb, carry, acc, _ = st
            k0 = pl.multiple_of(kb * KB, KB)
            z = lax.dot_general(q, k_ref[pl.ds(k0, KB), :], nt, preferred_element_type=F32) * scale
            p = _suffix_sums(-_softplus(z), mm)
            a = jnp.exp(z + p[:, :KB] + carry)
            acc = acc + jnp.dot(a.astype(BF16), v_ref[pl.ds(k0, KB), :], preferred_element_type=F32)
            carry = carry + p[:, KB:]
            return kb - 1, carry, acc, jnp.max(carry)

        _, _, acc, _ = lax.while_loop(cond, body, (i - 1, carry, acc, jnp.max(carry)))
        o_ref[pl.ds(r0, QB), :] = (acc * sg_ref[pl.ds(r0, QB), :]).astype(o_ref.dtype)
        return 0

    lax.fori_loop(0, s // QB, qblock, 0)


def _attention(q, k, v, sg):
    h, s, dh = q.shape
    head_spec = pl.BlockSpec((None, s, dh), lambda i: (i, 0, 0))
    block_bytes = s * dh * (3 * 2 + 4 + 2)
    return pl.pallas_call(
        _attn_kernel,
        grid=(h,),
        in_specs=[head_spec] * 4,
        out_specs=pl.BlockSpec((s, dh), lambda i: (0, i)),
        out_shape=jax.ShapeDtypeStruct((s, h * dh), BF16),
        scratch_shapes=[pltpu.VMEM((2 * KB, 2 * KB), BF16)],
        compiler_params=pltpu.CompilerParams(
            dimension_semantics=("arbitrary",),
            vmem_limit_bytes=_vmem_limit(block_bytes, 4 * KB * KB * 2, 4 << 20)),
        name="stickbreak_attn",
    )(q, k, v, sg)


def _outproj_kernel(a_ref, w_ref, res_ref, o_ref):
    o_ref[...] = res_ref[...] + jnp.dot(a_ref[...], w_ref[...], preferred_element_type=F32)


def _outproj_residual(a, w_bf16, res, tm=1024, tn=512):
    s, kdim = a.shape
    n = w_bf16.shape[1]
    block_bytes = tm * kdim * 2 + kdim * tn * 2 + 2 * tm * tn * 4
    return pl.pallas_call(
        _outproj_kernel,
        grid=(s // tm, n // tn),
        in_specs=[pl.BlockSpec((tm, kdim), lambda m, j: (m, 0)),
                  pl.BlockSpec((kdim, tn), lambda m, j: (0, j)),
                  pl.BlockSpec((tm, tn), lambda m, j: (m, j))],
        out_specs=pl.BlockSpec((tm, tn), lambda m, j: (m, j)),
        out_shape=jax.ShapeDtypeStruct((s, n), F32),
        compiler_params=pltpu.CompilerParams(
            dimension_semantics=("arbitrary", "arbitrary"),
            vmem_limit_bytes=_vmem_limit(block_bytes, 0, 2 * tm * tn * 4)),
        name="attn_outproj",
    )(a, w_bf16, res)


def _outproj_norm_kernel(a_ref, w_ref, res_ref, gain_ref, o_ref):
    kk = pl.program_id(1)

    @pl.when(kk == 0)
    def _():
        o_ref[...] = res_ref[...]

    o_ref[...] += jnp.dot(a_ref[...], w_ref[...], preferred_element_type=F32)

    @pl.when(kk == pl.num_programs(1) - 1)
    def _():
        _rmsnorm_rows(o_ref, gain_ref, o_ref, o_ref.shape[0])


def _outproj_residual_norm(a, w_bf16, res, gain, tm=512, tk=1024):
    s, kdim = a.shape
    n = w_bf16.shape[1]
    block_bytes = tm * tk * 2 + tk * n * 2 + 2 * tm * n * 4 + n * 4
    return pl.pallas_call(
        _outproj_norm_kernel,
        grid=(s // tm, kdim // tk),
        in_specs=[pl.BlockSpec((tm, tk), lambda m, kk: (m, kk)),
                  pl.BlockSpec((tk, n), lambda m, kk: (kk, 0)),
                  pl.BlockSpec((tm, n), lambda m, kk: (m, 0)),
                  pl.BlockSpec((1, n), lambda m, kk: (0, 0))],
        out_specs=pl.BlockSpec((tm, n), lambda m, kk: (m, 0)),
        out_shape=jax.ShapeDtypeStruct((s, n), F32),
        compiler_params=pltpu.CompilerParams(
            dimension_semantics=("arbitrary", "arbitrary"),
            vmem_limit_bytes=_vmem_limit(block_bytes, 0, tm * n * 4)),
        name="conv_outproj_norm",
    )(a, w_bf16, res, gain)


def kernel(x, norm_attn, w_in_attn, w_out_attn, norm_conv, w_in_conv, conv_w, w_out_conv, final_norm):
    b, s, d = x.shape
    assert b == 1 and d == N_HEADS * HEAD_DIM
    x2 = x.reshape(s, d)
    row = lambda g: g.reshape(1, d).astype(F32)

    q, k, v, sg = _attn_inproj(x2, row(norm_attn), w_in_attn.astype(BF16))
    og = _attention(q, k, v, sg)
    h1 = _outproj_residual(og, w_out_attn.astype(BF16), x2)
    yg = _conv_inproj(h1, row(norm_conv), w_in_conv.astype(BF16), conv_w.astype(F32))
    out = _outproj_residual_norm(yg, w_out_conv.astype(BF16), h1, row(final_norm))
    return out.reshape(b, s, d)
```

```python
import functools
import math

import jax
import jax.numpy as jnp
from jax import lax
from jax.experimental import pallas as pl
from jax.experimental.pallas import tpu as pltpu

F32 = jnp.float32
BF16 = jnp.bfloat16

RMS_EPS = 1e-6
N_HEADS = 32
HEAD_DIM = 128
CONV_K = 3

LANES = 128
SUBLANES = 8
VMEM_BYTES_V7X = 64 * 1024 * 1024

QB = 128
KB = 128
WIN = 3
HEADS_PER_STEP = 4
Q_ROWS_PER_STEP = 1024
LOG_ZERO_F32 = -105.0

NORM_ROWS = 16


def _vmem_limit(block_bytes, scratch_bytes, temp_bytes):
    need = 2 * block_bytes + scratch_bytes + temp_bytes + (4 << 20)
    return int(min(need, VMEM_BYTES_V7X - (4 << 20)))


def _silu(g):
    return g * (1.0 / (1.0 + jnp.exp(-g)))


def _rmsnorm_rows(x_ref, gain_ref, dst_ref, rows):
    def body(r, _):
        sl = pl.ds(pl.multiple_of(r * NORM_ROWS, NORM_ROWS), NORM_ROWS)
        xf = x_ref[sl, :]
        ms = jnp.mean(xf * xf, axis=-1, keepdims=True)
        dst_ref[sl, :] = (xf * lax.rsqrt(ms + RMS_EPS) * gain_ref[...]).astype(dst_ref.dtype)
        return 0
    lax.fori_loop(0, rows // NORM_ROWS, body, 0)


def _proj4(xn_ref, w_refs):
    xn = xn_ref[...]
    return [jnp.dot(xn, w[...], preferred_element_type=F32) for w in w_refs]


def _attn_inproj_kernel(x_ref, gain_ref, wq, wk, wv, wg, q_ref, k_ref, v_ref, sg_ref, xn_ref):
    tm = x_ref.shape[0]

    @pl.when(pl.program_id(1) == 0)
    def _():
        _rmsnorm_rows(x_ref, gain_ref, xn_ref, tm)

    q, k, v, g = _proj4(xn_ref, (wq, wk, wv, wg))
    sg = _silu(g)
    for hh in range(q_ref.shape[0]):
        cols = slice(hh * HEAD_DIM, (hh + 1) * HEAD_DIM)
        q_ref[hh] = q[:, cols].astype(BF16)
        k_ref[hh] = k[:, cols].astype(BF16)
        v_ref[hh] = v[:, cols].astype(BF16)
        sg_ref[hh] = sg[:, cols]


def _conv_inproj_kernel(x_ref, gain_ref, wb, wc, wu, wg, cw_ref, y_ref, xn_ref, halo_ref):
    tm = x_ref.shape[0]
    m = pl.program_id(0)
    c = pl.program_id(1)

    @pl.when(c == 0)
    def _():
        _rmsnorm_rows(x_ref, gain_ref, xn_ref, tm)

    gb, gc, u, g = _proj4(xn_ref, (wb, wc, wu, wg))
    cu = gc * u

    halo = jnp.where(m > 0, halo_ref[c], 0.0)
    prev1 = halo[SUBLANES - 1:SUBLANES, :]
    prev2 = halo[SUBLANES - 2:SUBLANES - 1, :]
    row = lax.broadcasted_iota(jnp.int32, cu.shape, 0)
    cu1 = jnp.where(row == 0, prev1, pltpu.roll(cu, 1, 0))
    cu2 = jnp.where(row == 0, prev2, jnp.where(row == 1, prev1, pltpu.roll(cu, 2, 0)))
    halo_ref[c] = cu[tm - SUBLANES:, :]

    cw = cw_ref[...]
    conv = cw[0:1, :] * cu2 + cw[1:2, :] * cu1 + cw[2:3, :] * cu
    y_ref[...] = ((gb * conv) * _silu(g)).astype(y_ref.dtype)


def _inproj_common(x, gain, w_bf16, tm, tn):
    s, d = x.shape
    width = w_bf16.shape[1] // 4
    nblk = width // tn
    x_spec = pl.BlockSpec((tm, d), lambda m, c: (m, 0))
    gain_spec = pl.BlockSpec((1, d), lambda m, c: (0, 0))
    w_specs = [pl.BlockSpec((d, tn), functools.partial(lambda m, c, j: (0, j * nblk + c), j=j))
               for j in range(4)]
    block_bytes = tm * d * 4 + d * 4 + 4 * d * tn * 2
    return s, d, width, nblk, x_spec, gain_spec, w_specs, block_bytes


def _attn_inproj(x, gain, w_bf16, tm=512, tn=256):
    s, d, width, nblk, x_spec, gain_spec, w_specs, block_bytes = _inproj_common(x, gain, w_bf16, tm, tn)
    hpt = tn // HEAD_DIM
    head_spec = pl.BlockSpec((hpt, tm, HEAD_DIM), lambda m, c: (c, m, 0))
    out_bytes = hpt * tm * HEAD_DIM * (3 * 2 + 4)
    return pl.pallas_call(
        _attn_inproj_kernel,
        grid=(s // tm, nblk),
        in_specs=[x_spec, gain_spec] + w_specs,
        out_specs=[head_spec] * 4,
        out_shape=[jax.ShapeDtypeStruct((N_HEADS, s, HEAD_DIM), BF16)] * 3
        + [jax.ShapeDtypeStruct((N_HEADS, s, HEAD_DIM), F32)],
        scratch_shapes=[pltpu.VMEM((tm, d), BF16)],
        compiler_params=pltpu.CompilerParams(
            dimension_semantics=("arbitrary", "arbitrary"),
            vmem_limit_bytes=_vmem_limit(block_bytes + out_bytes, tm * d * 2, 8 * tm * tn * 4)),
        name="attn_inproj",
    )(x, gain, w_bf16, w_bf16, w_bf16, w_bf16)


def _conv_inproj(x, gain, w_bf16, conv_w, tm=512, tn=256):
    s, d, width, nblk, x_spec, gain_spec, w_specs, block_bytes = _inproj_common(x, gain, w_bf16, tm, tn)
    cw_spec = pl.BlockSpec((CONV_K, tn), lambda m, c: (0, c))
    y_spec = pl.BlockSpec((tm, tn), lambda m, c: (m, c))
    return pl.pallas_call(
        _conv_inproj_kernel,
        grid=(s // tm, nblk),
        in_specs=[x_spec, gain_spec] + w_specs + [cw_spec],
        out_specs=y_spec,
        out_shape=jax.ShapeDtypeStruct((s, width), BF16),
        scratch_shapes=[pltpu.VMEM((tm, d), BF16), pltpu.VMEM((nblk, SUBLANES, tn), F32)],
        compiler_params=pltpu.CompilerParams(
            dimension_semantics=("arbitrary", "arbitrary"),
            vmem_limit_bytes=_vmem_limit(block_bytes + tm * tn * 2, tm * d * 2, 10 * tm * tn * 4)),
        name="conv_inproj",
    )(x, gain, w_bf16, w_bf16, w_bf16, w_bf16, conv_w)


def _softplus(z):
    return jnp.maximum(z, 0.0) + jnp.log(1.0 + jnp.exp(-jnp.abs(z)))


def _split_hi_lo(lf):
    hi = lf.astype(BF16)
    lo = (lf - hi.astype(F32)).astype(BF16)
    return jnp.concatenate([hi, lo], axis=1)


def _attn_kernel(q_ref, k_ref, v_ref, sg_ref, o_ref, mm_ref):
    n_heads, sq, dh = q_ref.shape
    nqb = sq // QB
    first_block = pl.program_id(1) * nqb
    scale = 1.0 / math.sqrt(HEAD_DIM)
    nt = (((1,), (1,)), ((), ()))

    r = lax.broadcasted_iota(jnp.int32, (2 * KB, 2 * KB), 0) & (KB - 1)
    cc = lax.broadcasted_iota(jnp.int32, (2 * KB, 2 * KB), 1)
    mm_ref[...] = jnp.where((cc >= KB) | (r >= cc), 1.0, 0.0).astype(BF16)

    def scores(q, kblk):
        return lax.dot_general(q, kblk, nt, preferred_element_type=F32) * scale

    def causal_mask():
        row = lax.broadcasted_iota(jnp.int32, (QB, KB), 0)
        col = lax.broadcasted_iota(jnp.int32, (QB, KB), 1)
        return col < row

    def windows(qs, ib, tri, nblk):
        heads = range(n_heads)
        ws = pl.multiple_of((ib - (nblk - 1)) * KB, KB)
        zs = [scores(qs[g], k_ref[g, pl.ds(ws, nblk * KB), :]) for g in heads]
        lhs = []
        for g in heads:
            nsp = -_softplus(zs[g])
            lfs = [nsp[:, b * KB:(b + 1) * KB] for b in range(nblk)]
            lfs[-1] = jnp.where(tri, lfs[-1], 0.0)
            lhs.append(jnp.concatenate([_split_hi_lo(lf) for lf in lfs], axis=0))
        ps = [jnp.dot(lhs[g], mm_ref[...], preferred_element_type=F32) for g in heads]
        a_all, carries = [], []
        for g in heads:
            carry = None
            a_blocks = [None] * nblk
            for b in reversed(range(nblk)):
                pb = ps[g][b * QB:(b + 1) * QB]
                log_a = zs[g][:, b * KB:(b + 1) * KB] + pb[:, :KB]
                if carry is not None:
                    log_a = log_a + carry
                a = jnp.exp(log_a)
                if b == nblk - 1:
                    a = jnp.where(tri, a, 0.0)
                a_blocks[b] = a.astype(BF16)
                carry = pb[:, KB:] if carry is None else carry + pb[:, KB:]
            a_all.append(jnp.concatenate(a_blocks, axis=1))
            carries.append(carry)
        accs = [jnp.dot(a_all[g], v_ref[g, pl.ds(ws, nblk * KB), :], preferred_element_type=F32)
                for g in heads]
        return list(zip(carries, accs))

    def sweep_rest(g, q, kb, carry, acc):
        def cond(st):
            kb, _, _, mx = st
            return jnp.logical_and(kb >= 0, mx > LOG_ZERO_F32)

        def body(st):
            kb, carry, acc, _ = st
            k0 = pl.multiple_of(kb * KB, KB)
            z = scores(q, k_ref[g, pl.ds(k0, KB), :])
            p = jnp.dot(_split_hi_lo(-_softplus(z)), mm_ref[...], preferred_element_type=F32)
            a = jnp.exp(z + p[:, :KB] + carry)
            acc = acc + jnp.dot(a.astype(BF16), v_ref[g, pl.ds(k0, KB), :], preferred_element_type=F32)
            carry = carry + p[:, KB:]
            return kb - 1, carry, acc, jnp.max(carry)

        _, _, acc, _ = lax.while_loop(cond, body, (kb, carry, acc, jnp.max(carry)))
        return acc

    def qblocks(il, nblk):
        r0 = il * QB if isinstance(il, int) else pl.multiple_of(il * QB, QB)
        ib = first_block + il
        tri = causal_mask()
        qs = [q_ref[g, pl.ds(r0, QB), :] for g in range(n_heads)]
        res = windows(qs, ib, tri, nblk)
        live = jnp.max(functools.reduce(jnp.maximum, [c for c, _ in res])) > LOG_ZERO_F32
        accs = lax.cond(
            live,
            lambda: [sweep_rest(g, qs[g], ib - nblk, *res[g]) for g in range(n_heads)],
            lambda: [a for _, a in res])
        for g in range(n_heads):
            gated = accs[g] * sg_ref[g, pl.ds(r0, QB), :]
            o_ref[pl.ds(r0, QB), g * dh:(g + 1) * dh] = gated.astype(o_ref.dtype)

    @pl.when(first_block == 0)
    def _():
        for il in range(WIN - 1):
            qblocks(il, il + 1)

    def body(il, _):
        qblocks(il, WIN)
        return 0

    lax.fori_loop(jnp.where(first_block == 0, WIN - 1, 0), nqb, body, 0)


def _attention(q, k, v, sg, heads=HEADS_PER_STEP, sq=Q_ROWS_PER_STEP):
    h, s, dh = q.shape
    q_spec = pl.BlockSpec((heads, sq, dh), lambda hg, j: (hg, j, 0))
    kv_spec = pl.BlockSpec((heads, s, dh), lambda hg, j: (hg, 0, 0))
    block_bytes = heads * dh * (sq * (2 + 4 + 2) + 2 * s * 2)
    return pl.pallas_call(
        _attn_kernel,
        grid=(h // heads, s // sq),
        in_specs=[q_spec, kv_spec, kv_spec, q_spec],
        out_specs=pl.BlockSpec((sq, heads * dh), lambda hg, j: (j, hg)),
        out_shape=jax.ShapeDtypeStruct((s, h * dh), BF16),
        scratch_shapes=[pltpu.VMEM((2 * KB, 2 * KB), BF16)],
        compiler_params=pltpu.CompilerParams(
            dimension_semantics=("arbitrary", "arbitrary"),
            vmem_limit_bytes=_vmem_limit(block_bytes, 4 * KB * KB * 2, 8 << 20)),
        name="stickbreak_attn",
    )(q, k, v, sg)


def _outproj_kernel(a_ref, w_ref, res_ref, o_ref):
    o_ref[...] = res_ref[...] + jnp.dot(a_ref[...], w_ref[...], preferred_element_type=F32)


def _outproj_residual(a, w_bf16, res, tm=1024, tn=512):
    s, kdim = a.shape
    n = w_bf16.shape[1]
    block_bytes = tm * kdim * 2 + kdim * tn * 2 + 2 * tm * tn * 4
    return pl.pallas_call(
        _outproj_kernel,
        grid=(s // tm, n // tn),
        in_specs=[pl.BlockSpec((tm, kdim), lambda m, j: (m, 0)),
                  pl.BlockSpec((kdim, tn), lambda m, j: (0, j)),
                  pl.BlockSpec((tm, tn), lambda m, j: (m, j))],
        out_specs=pl.BlockSpec((tm, tn), lambda m, j: (m, j)),
        out_shape=jax.ShapeDtypeStruct((s, n), F32),
        compiler_params=pltpu.CompilerParams(
            dimension_semantics=("arbitrary", "arbitrary"),
            vmem_limit_bytes=_vmem_limit(block_bytes, 0, 2 * tm * tn * 4)),
        name="attn_outproj",
    )(a, w_bf16, res)


def _outproj_norm_kernel(a_ref, w_ref, res_ref, gain_ref, o_ref):
    kk = pl.program_id(1)

    @pl.when(kk == 0)
    def _():
        o_ref[...] = res_ref[...]

    o_ref[...] += jnp.dot(a_ref[...], w_ref[...], preferred_element_type=F32)

    @pl.when(kk == pl.num_programs(1) - 1)
    def _():
        _rmsnorm_rows(o_ref, gain_ref, o_ref, o_ref.shape[0])


def _outproj_residual_norm(a, w_bf16, res, gain, tm=512, tk=1024):
    s, kdim = a.shape
    n = w_bf16.shape[1]
    block_bytes = tm * tk * 2 + tk * n * 2 + 2 * tm * n * 4 + n * 4
    return pl.pallas_call(
        _outproj_norm_kernel,
        grid=(s // tm, kdim // tk),
        in_specs=[pl.BlockSpec((tm, tk), lambda m, kk: (m, kk)),
                  pl.BlockSpec((tk, n), lambda m, kk: (kk, 0)),
                  pl.BlockSpec((tm, n), lambda m, kk: (m, 0)),
                  pl.BlockSpec((1, n), lambda m, kk: (0, 0))],
        out_specs=pl.BlockSpec((tm, n), lambda m, kk: (m, 0)),
        out_shape=jax.ShapeDtypeStruct((s, n), F32),
        compiler_params=pltpu.CompilerParams(
            dimension_semantics=("arbitrary", "arbitrary"),
            vmem_limit_bytes=_vmem_limit(block_bytes, 0, tm * n * 4)),
        name="conv_outproj_norm",
    )(a, w_bf16, res, gain)


def kernel(x, norm_attn, w_in_attn, w_out_attn, norm_conv, w_in_conv, conv_w, w_out_conv, final_norm):
    b, s, d = x.shape
    assert b == 1 and d == N_HEADS * HEAD_DIM
    x2 = x.reshape(s, d)
    row = lambda g: g.reshape(1, d).astype(F32)

    q, k, v, sg = _attn_inproj(x2, row(norm_attn), w_in_attn.astype(BF16))
    og = _attention(q, k, v, sg)
    h1 = _outproj_residual(og, w_out_attn.astype(BF16), x2)
    yg = _conv_inproj(h1, row(norm_conv), w_in_conv.astype(BF16), conv_w.astype(F32))
    out = _outproj_residual_norm(yg, w_out_conv.astype(BF16), h1, row(final_norm))
    return out.reshape(b, s, d)
```

```python
import functools
import math

import jax
import jax.numpy as jnp
from jax import lax
from jax.experimental import pallas as pl
from jax.experimental.pallas import tpu as pltpu

F32 = jnp.float32
BF16 = jnp.bfloat16

RMS_EPS = 1e-6
N_HEADS = 32
HEAD_DIM = 128
CONV_K = 3

LANES = 128
SUBLANES = 8
VMEM_BYTES_V7X = 64 * 1024 * 1024

QB = 128
KB = 128
WIN = 3
HEADS_PER_STEP = 4
Q_ROWS_PER_STEP = 1024
Q_BLOCKS_PER_ITER = 2
LOG2_E = math.log2(math.e)
LOG2_ZERO_F32 = -152.0

NORM_ROWS = 16
NORM_UNROLL = 4


def _vmem_limit(block_bytes, scratch_bytes, temp_bytes):
    need = 2 * block_bytes + scratch_bytes + temp_bytes + (4 << 20)
    return int(min(need, VMEM_BYTES_V7X - (4 << 20)))


def _silu(g):
    return g * (1.0 / (1.0 + jnp.exp(-g)))


def _rmsnorm_rows(x_ref, gain_ref, dst_ref, rows):
    group = NORM_ROWS * NORM_UNROLL

    def body(r, _):
        base = r * group
        sls = [pl.ds(pl.multiple_of(base + j * NORM_ROWS, NORM_ROWS), NORM_ROWS) for j in range(NORM_UNROLL)]
        inv = []
        for sl in sls:
            xf = x_ref[sl, :]
            inv.append(lax.rsqrt(jnp.mean(xf * xf, axis=-1, keepdims=True) + RMS_EPS))
        for sl, rs in zip(sls, inv):
            dst_ref[sl, :] = (x_ref[sl, :] * rs * gain_ref[...]).astype(dst_ref.dtype)
        return 0
    lax.fori_loop(0, rows // group, body, 0)


def _norm_kernel(x_ref, gain_ref, o_ref):
    _rmsnorm_rows(x_ref, gain_ref, o_ref, x_ref.shape[0])


def _rmsnorm_bf16(x, gain, tm=256):
    s, d = x.shape
    return pl.pallas_call(
        _norm_kernel,
        grid=(s // tm,),
        in_specs=[pl.BlockSpec((tm, d), lambda m: (m, 0)), pl.BlockSpec((1, d), lambda m: (0, 0))],
        out_specs=pl.BlockSpec((tm, d), lambda m: (m, 0)),
        out_shape=jax.ShapeDtypeStruct((s, d), BF16),
        compiler_params=pltpu.CompilerParams(
            dimension_semantics=("arbitrary",),
            vmem_limit_bytes=_vmem_limit(tm * d * (4 + 2) + d * 4, 0, 4 << 20)),
        name="rmsnorm_bf16",
    )(x, gain)


def _proj4(xn_ref, w_refs):
    xn = xn_ref[...]
    return [jnp.dot(xn, w[...].astype(BF16), preferred_element_type=F32) for w in w_refs]


def _attn_inproj_kernel(xn_ref, wq, wk, wv, wg, q_ref, k_ref, v_ref, sg_ref):
    q, k, v, g = _proj4(xn_ref, (wq, wk, wv, wg))
    sg = _silu(g)
    for hh in range(q_ref.shape[0]):
        cols = slice(hh * HEAD_DIM, (hh + 1) * HEAD_DIM)
        q_ref[hh] = q[:, cols].astype(BF16)
        k_ref[hh] = k[:, cols].astype(BF16)
        v_ref[hh] = v[:, cols].astype(BF16)
        sg_ref[hh] = sg[:, cols]


def _conv_inproj_kernel(xn_ref, wb, wc, wu, wg, cw_ref, y_ref, halo_ref):
    tm = xn_ref.shape[0]
    m = pl.program_id(1)

    gc, u = _proj4(xn_ref, (wc, wu))
    cu = gc * u

    halo = jnp.where(m > 0, halo_ref[...], 0.0)
    prev1 = halo[SUBLANES - 1:SUBLANES, :]
    prev2 = halo[SUBLANES - 2:SUBLANES - 1, :]
    row = lax.broadcasted_iota(jnp.int32, cu.shape, 0)
    cu1 = jnp.where(row == 0, prev1, pltpu.roll(cu, 1, 0))
    cu2 = jnp.where(row == 0, prev2, jnp.where(row == 1, prev1, pltpu.roll(cu, 2, 0)))
    halo_ref[...] = cu[tm - SUBLANES:, :]

    cw = cw_ref[...]
    conv = cw[0:1, :] * cu2 + cw[1:2, :] * cu1 + cw[2:3, :] * cu
    (gb,) = _proj4(xn_ref, (wb,))
    y = gb * conv
    (g,) = _proj4(xn_ref, (wg,))
    y_ref[...] = (y * _silu(g)).astype(y_ref.dtype)


def _inproj_common(xn, w, tm, tn):
    s, d = xn.shape
    width = w.shape[1] // 4
    nblk = width // tn
    x_spec = pl.BlockSpec((tm, d), lambda c, m: (m, 0))
    w_specs = [pl.BlockSpec((d, tn), functools.partial(lambda c, m, j: (0, j * nblk + c), j=j))
               for j in range(4)]
    block_bytes = tm * d * 2 + 4 * d * tn * 4
    return s, d, width, nblk, x_spec, w_specs, block_bytes


def _attn_inproj(xn, w, tm=512, tn=256):
    s, d, width, nblk, x_spec, w_specs, block_bytes = _inproj_common(xn, w, tm, tn)
    hpt = tn // HEAD_DIM
    head_spec = pl.BlockSpec((hpt, tm, HEAD_DIM), lambda c, m: (c, m, 0))
    out_bytes = hpt * tm * HEAD_DIM * (3 * 2 + 4)
    return pl.pallas_call(
        _attn_inproj_kernel,
        grid=(nblk, s // tm),
        in_specs=[x_spec] + w_specs,
        out_specs=[head_spec] * 4,
        out_shape=[jax.ShapeDtypeStruct((N_HEADS, s, HEAD_DIM), BF16)] * 3
        + [jax.ShapeDtypeStruct((N_HEADS, s, HEAD_DIM), F32)],
        compiler_params=pltpu.CompilerParams(
            dimension_semantics=("arbitrary", "arbitrary"),
            vmem_limit_bytes=_vmem_limit(block_bytes + out_bytes, 0, 8 * tm * tn * 4 + 4 * d * tn * 2)),
        name="attn_inproj",
    )(xn, w, w, w, w)


def _conv_inproj(xn, w, conv_w, tm=512, tn=256):
    s, d, width, nblk, x_spec, w_specs, block_bytes = _inproj_common(xn, w, tm, tn)
    cw_spec = pl.BlockSpec((CONV_K, tn), lambda c, m: (0, c))
    y_spec = pl.BlockSpec((tm, tn), lambda c, m: (m, c))
    return pl.pallas_call(
        _conv_inproj_kernel,
        grid=(nblk, s // tm),
        in_specs=[x_spec] + w_specs + [cw_spec],
        out_specs=y_spec,
        out_shape=jax.ShapeDtypeStruct((s, width), BF16),
        scratch_shapes=[pltpu.VMEM((SUBLANES, tn), F32)],
        compiler_params=pltpu.CompilerParams(
            dimension_semantics=("arbitrary", "arbitrary"),
            vmem_limit_bytes=_vmem_limit(block_bytes + tm * tn * 2, 0, 10 * tm * tn * 4 + 4 * d * tn * 2)),
        name="conv_inproj",
    )(xn, w, w, w, w, conv_w)


def _softplus2(z2):
    return jnp.maximum(z2, 0.0) + jnp.log(1.0 + jnp.exp2(-jnp.abs(z2))) * LOG2_E


def _split_hi_lo(x):
    hi = x.astype(BF16)
    lo = (x - hi.astype(F32)).astype(BF16)
    return jnp.concatenate([hi, lo], axis=1)


def _attn_kernel(q_ref, k_ref, v_ref, sg_ref, o_ref, mm_ref):
    n_heads, sq, dh = q_ref.shape
    nqb = sq // QB
    first_block = pl.program_id(1) * nqb
    scale2 = LOG2_E / math.sqrt(HEAD_DIM)
    nt = (((1,), (1,)), ((), ()))

    r = lax.broadcasted_iota(jnp.int32, (2 * KB, 2 * KB), 0) & (KB - 1)
    cc = lax.broadcasted_iota(jnp.int32, (2 * KB, 2 * KB), 1)
    mm_ref[...] = jnp.where((cc >= KB) | (r >= cc), -1.0, 0.0).astype(BF16)

    def scores(q, kblk):
        return lax.dot_general(q, kblk, nt, preferred_element_type=F32) * scale2

    def causal_mask():
        row = lax.broadcasted_iota(jnp.int32, (QB, KB), 0)
        col = lax.broadcasted_iota(jnp.int32, (QB, KB), 1)
        return col < row

    def windows(chains, tri, nblk):
        wss = [pl.multiple_of((ib - (nblk - 1)) * KB, KB) for _, _, ib in chains]
        zs = [scores(q_ref[g, pl.ds(r0, QB), :], k_ref[g, pl.ds(ws, nblk * KB), :])
              for (g, r0, _), ws in zip(chains, wss)]
        lhs = []
        for z in zs:
            sp = _softplus2(z)
            sps = [sp[:, b * KB:(b + 1) * KB] for b in range(nblk)]
            sps[-1] = jnp.where(tri, sps[-1], 0.0)
            lhs.append(jnp.concatenate([_split_hi_lo(s) for s in sps], axis=0))
        ps = [jnp.dot(l, mm_ref[...], preferred_element_type=F32) for l in lhs]
        a_all, carries = [], []
        for z, p in zip(zs, ps):
            carry = None
            a_blocks = [None] * nblk
            for b in reversed(range(nblk)):
                pb = p[b * QB:(b + 1) * QB]
                log2_a = z[:, b * KB:(b + 1) * KB] + pb[:, :KB]
                if carry is not None:
                    log2_a = log2_a + carry
                a = jnp.exp2(log2_a)
                if b == nblk - 1:
                    a = jnp.where(tri, a, 0.0)
                a_blocks[b] = a.astype(BF16)
                carry = pb[:, KB:] if carry is None else carry + pb[:, KB:]
            a_all.append(jnp.concatenate(a_blocks, axis=1))
            carries.append(carry)
        accs = [jnp.dot(a, v_ref[g, pl.ds(ws, nblk * KB), :], preferred_element_type=F32)
                for a, (g, _, _), ws in zip(a_all, chains, wss)]
        return list(zip(carries, accs))

    def sweep_rest(g, r0, kb, carry, acc):
        q = q_ref[g, pl.ds(r0, QB), :]

        def cond(st):
            kb, _, _, mx = st
            return jnp.logical_and(kb >= 0, mx > LOG2_ZERO_F32)

        def body(st):
            kb, carry, acc, _ = st
            k0 = pl.multiple_of(kb * KB, KB)
            z = scores(q, k_ref[g, pl.ds(k0, KB), :])
            p = jnp.dot(_split_hi_lo(_softplus2(z)), mm_ref[...], preferred_element_type=F32)
            a = jnp.exp2(z + p[:, :KB] + carry)
            acc = acc + jnp.dot(a.astype(BF16), v_ref[g, pl.ds(k0, KB), :], preferred_element_type=F32)
            carry = carry + p[:, KB:]
            return kb - 1, carry, acc, jnp.max(carry)

        _, _, acc, _ = lax.while_loop(cond, body, (kb, carry, acc, jnp.max(carry)))
        return acc

    def qblocks(ils, nblk):
        chains = []
        for il in ils:
            r0 = il * QB if isinstance(il, int) else pl.multiple_of(il * QB, QB)
            chains += [(g, r0, first_block + il) for g in range(n_heads)]
        res = windows(chains, causal_mask(), nblk)
        live = jnp.max(functools.reduce(jnp.maximum, [c for c, _ in res])) > LOG2_ZERO_F32
        accs = lax.cond(
            live,
            lambda: [sweep_rest(g, r0, ib - nblk, *rs) for (g, r0, ib), rs in zip(chains, res)],
            lambda: [a for _, a in res])
        for (g, r0, _), acc in zip(chains, accs):
            gated = acc * sg_ref[g, pl.ds(r0, QB), :]
            o_ref[pl.ds(r0, QB), g * dh:(g + 1) * dh] = gated.astype(o_ref.dtype)

    @pl.when(first_block == 0)
    def _():
        for il in range(WIN - 1):
            qblocks([il], il + 1)

    def body(it, _):
        qblocks([it * Q_BLOCKS_PER_ITER + j for j in range(Q_BLOCKS_PER_ITER)], WIN)
        return 0

    assert (WIN - 1) % Q_BLOCKS_PER_ITER == 0 and nqb % Q_BLOCKS_PER_ITER == 0
    lax.fori_loop(jnp.where(first_block == 0, (WIN - 1) // Q_BLOCKS_PER_ITER, 0),
                  nqb // Q_BLOCKS_PER_ITER, body, 0)


def _attention(q, k, v, sg, heads=HEADS_PER_STEP, sq=Q_ROWS_PER_STEP):
    h, s, dh = q.shape
    q_spec = pl.BlockSpec((heads, sq, dh), lambda hg, j: (hg, j, 0))
    kv_spec = pl.BlockSpec((heads, s, dh), lambda hg, j: (hg, 0, 0))
    block_bytes = heads * dh * (sq * (2 + 4 + 2) + 2 * s * 2)
    return pl.pallas_call(
        _attn_kernel,
        grid=(h // heads, s // sq),
        in_specs=[q_spec, kv_spec, kv_spec, q_spec],
        out_specs=pl.BlockSpec((sq, heads * dh), lambda hg, j: (j, hg)),
        out_shape=jax.ShapeDtypeStruct((s, h * dh), BF16),
        scratch_shapes=[pltpu.VMEM((2 * KB, 2 * KB), BF16)],
        compiler_params=pltpu.CompilerParams(
            dimension_semantics=("arbitrary", "arbitrary"),
            vmem_limit_bytes=_vmem_limit(block_bytes, 4 * KB * KB * 2, 8 << 20)),
        name="stickbreak_attn",
    )(q, k, v, sg)


def _outproj_kernel(a_ref, w_ref, res_ref, o_ref):
    o_ref[...] = res_ref[...] + jnp.dot(a_ref[...], w_ref[...].astype(BF16), preferred_element_type=F32)


def _outproj_residual(a, w, res, tm=1024, tn=512):
    s, kdim = a.shape
    n = w.shape[1]
    block_bytes = tm * kdim * 2 + kdim * tn * 4 + 2 * tm * tn * 4
    return pl.pallas_call(
        _outproj_kernel,
        grid=(s // tm, n // tn),
        in_specs=[pl.BlockSpec((tm, kdim), lambda m, j: (m, 0)),
                  pl.BlockSpec((kdim, tn), lambda m, j: (0, j)),
                  pl.BlockSpec((tm, tn), lambda m, j: (m, j))],
        out_specs=pl.BlockSpec((tm, tn), lambda m, j: (m, j)),
        out_shape=jax.ShapeDtypeStruct((s, n), F32),
        compiler_params=pltpu.CompilerParams(
            dimension_semantics=("arbitrary", "arbitrary"),
            vmem_limit_bytes=_vmem_limit(block_bytes, 0, 2 * tm * tn * 4 + kdim * tn * 2)),
        name="attn_outproj",
    )(a, w, res)


def _outproj_norm_kernel(a_ref, w_ref, res_ref, gain_ref, o_ref):
    kk = pl.program_id(1)

    @pl.when(kk == 0)
    def _():
        o_ref[...] = res_ref[...]

    o_ref[...] += jnp.dot(a_ref[...], w_ref[...], preferred_element_type=F32)

    @pl.when(kk == pl.num_programs(1) - 1)
    def _():
        _rmsnorm_rows(o_ref, gain_ref, o_ref, o_ref.shape[0])


def _outproj_residual_norm(a, w_bf16, res, gain, tm=512, tk=1024):
    s, kdim = a.shape
    n = w_bf16.shape[1]
    block_bytes = tm * tk * 2 + tk * n * 2 + 2 * tm * n * 4 + n * 4
    return pl.pallas_call(
        _outproj_norm_kernel,
        grid=(s // tm, kdim // tk),
        in_specs=[pl.BlockSpec((tm, tk), lambda m, kk: (m, kk)),
                  pl.BlockSpec((tk, n), lambda m, kk: (kk, 0)),
                  pl.BlockSpec((tm, n), lambda m, kk: (m, 0)),
                  pl.BlockSpec((1, n), lambda m, kk: (0, 0))],
        out_specs=pl.BlockSpec((tm, n), lambda m, kk: (m, 0)),
        out_shape=jax.ShapeDtypeStruct((s, n), F32),
        compiler_params=pltpu.CompilerParams(
            dimension_semantics=("arbitrary", "arbitrary"),
            vmem_limit_bytes=_vmem_limit(block_bytes, 0, tm * n * 4)),
        name="conv_outproj_norm",
    )(a, w_bf16, res, gain)


def kernel(x, norm_attn, w_in_attn, w_out_attn, norm_conv, w_in_conv, conv_w, w_out_conv, final_norm):
    b, s, d = x.shape
    assert b == 1 and d == N_HEADS * HEAD_DIM
    x2 = x.reshape(s, d)
    row = lambda g: g.reshape(1, d).astype(F32)

    q, k, v, sg = _attn_inproj(_rmsnorm_bf16(x2, row(norm_attn)), w_in_attn)
    og = _attention(q, k, v, sg)
    h1 = _outproj_residual(og, w_out_attn, x2)
    yg = _conv_inproj(_rmsnorm_bf16(h1, row(norm_conv)), w_in_conv, conv_w.astype(F32))
    out = _outproj_residual_norm(yg, w_out_conv.astype(BF16), h1, row(final_norm))
    return out.reshape(b, s, d)
```

```python
import functools
import math

import jax
import jax.numpy as jnp
from jax import lax
from jax.experimental import pallas as pl
from jax.experimental.pallas import tpu as pltpu

F32 = jnp.float32
BF16 = jnp.bfloat16

RMS_EPS = 1e-6
N_HEADS = 32
HEAD_DIM = 128
CONV_K = 3

LANES = 128
SUBLANES = 8
VMEM_BYTES_V7X = 64 * 1024 * 1024

QB = 128
KB = 128
WIN = 3
HEADS_PER_STEP = 4
Q_ROWS_PER_STEP = 1024
Q_BLOCKS_PER_ITER = 2
LOG2_E = math.log2(math.e)
LOG2_ZERO_F32 = -152.0

NORM_ROWS = 16
NORM_UNROLL = 4


def _vmem_limit(block_bytes, scratch_bytes, temp_bytes):
    need = 2 * block_bytes + scratch_bytes + temp_bytes + (4 << 20)
    return int(min(need, VMEM_BYTES_V7X - (4 << 20)))


def _silu(g):
    return g * (1.0 / (1.0 + jnp.exp(-g)))


def _rmsnorm_rows(x_ref, gain_ref, dst_ref, rows):
    group = NORM_ROWS * NORM_UNROLL

    def body(r, _):
        base = r * group
        sls = [pl.ds(pl.multiple_of(base + j * NORM_ROWS, NORM_ROWS), NORM_ROWS) for j in range(NORM_UNROLL)]
        inv = []
        for sl in sls:
            xf = x_ref[sl, :]
            inv.append(lax.rsqrt(jnp.mean(xf * xf, axis=-1, keepdims=True) + RMS_EPS))
        for sl, rs in zip(sls, inv):
            dst_ref[sl, :] = (x_ref[sl, :] * rs * gain_ref[...]).astype(dst_ref.dtype)
        return 0
    lax.fori_loop(0, rows // group, body, 0)


def _norm_kernel(x_ref, gain_ref, o_ref):
    _rmsnorm_rows(x_ref, gain_ref, o_ref, x_ref.shape[0])


def _rmsnorm_bf16(x, gain, tm=256):
    s, d = x.shape
    return pl.pallas_call(
        _norm_kernel,
        grid=(s // tm,),
        in_specs=[pl.BlockSpec((tm, d), lambda m: (m, 0)), pl.BlockSpec((1, d), lambda m: (0, 0))],
        out_specs=pl.BlockSpec((tm, d), lambda m: (m, 0)),
        out_shape=jax.ShapeDtypeStruct((s, d), BF16),
        compiler_params=pltpu.CompilerParams(
            dimension_semantics=("arbitrary",),
            vmem_limit_bytes=_vmem_limit(tm * d * (4 + 2) + d * 4, 0, 4 << 20)),
        name="rmsnorm_bf16",
    )(x, gain)


N_PROJ = 4
CAST_ROWS = 256


def _stage_weights(w_hbm, stage_ref, wbf_ref, sem, c, nblk):
    tn = stage_ref.shape[2]
    width = nblk * tn

    def copy(j, cc):
        col = pl.multiple_of(j * width + cc * tn, tn)
        return pltpu.make_async_copy(w_hbm.at[:, pl.ds(col, tn)], stage_ref.at[j], sem.at[j])

    @pl.when(c == 0)
    def _():
        for j in range(N_PROJ):
            copy(j, c).start()

    for j in range(N_PROJ):
        copy(j, c).wait()

    def cast(r, _):
        rows = pl.ds(pl.multiple_of(r * CAST_ROWS, CAST_ROWS), CAST_ROWS)
        for j in range(N_PROJ):
            wbf_ref[j, rows, :] = stage_ref[j, rows, :].astype(BF16)
        return 0
    lax.fori_loop(0, stage_ref.shape[1] // CAST_ROWS, cast, 0)

    @pl.when(c + 1 < nblk)
    def _():
        for j in range(N_PROJ):
            copy(j, c + 1).start()


def _proj(xn_ref, wbf_ref, j):
    return jnp.dot(xn_ref[...], wbf_ref[j], preferred_element_type=F32)


def _attn_inproj_kernel(xn_ref, w_hbm, q_ref, k_ref, v_ref, sg_ref, stage_ref, wbf_ref, sem):
    @pl.when(pl.program_id(1) == 0)
    def _():
        _stage_weights(w_hbm, stage_ref, wbf_ref, sem, pl.program_id(0), pl.num_programs(0))

    heads = [slice(hh * HEAD_DIM, (hh + 1) * HEAD_DIM) for hh in range(q_ref.shape[0])]
    sg = _silu(_proj(xn_ref, wbf_ref, 3))
    for hh, cols in enumerate(heads):
        sg_ref[hh] = sg[:, cols]
    for j, o_ref in enumerate((q_ref, k_ref, v_ref)):
        p = _proj(xn_ref, wbf_ref, j)
        for hh, cols in enumerate(heads):
            o_ref[hh] = p[:, cols].astype(BF16)


def _conv_inproj_kernel(xn_ref, w_hbm, cw_ref, y_ref, stage_ref, wbf_ref, sem, halo_ref):
    tm = xn_ref.shape[0]
    m = pl.program_id(1)

    @pl.when(m == 0)
    def _():
        _stage_weights(w_hbm, stage_ref, wbf_ref, sem, pl.program_id(0), pl.num_programs(0))

    sg = _silu(_proj(xn_ref, wbf_ref, 3))
    cu = _proj(xn_ref, wbf_ref, 1) * _proj(xn_ref, wbf_ref, 2)

    halo = jnp.where(m > 0, halo_ref[...], 0.0)
    prev1 = halo[SUBLANES - 1:SUBLANES, :]
    prev2 = halo[SUBLANES - 2:SUBLANES - 1, :]
    row = lax.broadcasted_iota(jnp.int32, cu.shape, 0)
    cu1 = jnp.where(row == 0, prev1, pltpu.roll(cu, 1, 0))
    cu2 = jnp.where(row == 0, prev2, jnp.where(row == 1, prev1, pltpu.roll(cu, 2, 0)))
    halo_ref[...] = cu[tm - SUBLANES:, :]

    cw = cw_ref[...]
    conv = cw[0:1, :] * cu2 + cw[1:2, :] * cu1 + cw[2:3, :] * cu
    gb = _proj(xn_ref, wbf_ref, 0)
    y_ref[...] = ((gb * conv) * sg).astype(y_ref.dtype)


def _inproj_call(kernel_fn, name, xn, w, extra_in, extra_specs, out_specs, out_shape, out_block_bytes,
                 extra_scratch, tm, tn):
    s, d = xn.shape
    nblk = w.shape[1] // N_PROJ // tn
    x_spec = pl.BlockSpec((tm, d), lambda c, m: (m, 0))
    w_spec = pl.BlockSpec(memory_space=pl.ANY)
    scratch = [pltpu.VMEM((N_PROJ, d, tn), F32), pltpu.VMEM((N_PROJ, d, tn), BF16),
               pltpu.SemaphoreType.DMA((N_PROJ,))] + extra_scratch
    scratch_bytes = N_PROJ * d * tn * (4 + 2)
    temp_bytes = 8 * tm * tn * 4
    return pl.pallas_call(
        kernel_fn,
        grid=(nblk, s // tm),
        in_specs=[x_spec, w_spec] + extra_specs,
        out_specs=out_specs,
        out_shape=out_shape,
        scratch_shapes=scratch,
        compiler_params=pltpu.CompilerParams(
            dimension_semantics=("arbitrary", "arbitrary"),
            vmem_limit_bytes=_vmem_limit(tm * d * 2 + out_block_bytes, scratch_bytes, temp_bytes)),
        name=name,
    )(xn, w, *extra_in)


def _attn_inproj(xn, w, tm=1024, tn=256):
    s = xn.shape[0]
    hpt = tn // HEAD_DIM
    head_spec = pl.BlockSpec((hpt, tm, HEAD_DIM), lambda c, m: (c, m, 0))
    return _inproj_call(
        _attn_inproj_kernel, "attn_inproj", xn, w, [], [],
        [head_spec] * 4,
        [jax.ShapeDtypeStruct((N_HEADS, s, HEAD_DIM), BF16)] * 3
        + [jax.ShapeDtypeStruct((N_HEADS, s, HEAD_DIM), F32)],
        hpt * tm * HEAD_DIM * (3 * 2 + 4), [], tm, tn)


def _conv_inproj(xn, w, conv_w, tm=1024, tn=256):
    s = xn.shape[0]
    width = w.shape[1] // N_PROJ
    return _inproj_call(
        _conv_inproj_kernel, "conv_inproj", xn, w, [conv_w],
        [pl.BlockSpec((CONV_K, tn), lambda c, m: (0, c))],
        pl.BlockSpec((tm, tn), lambda c, m: (m, c)),
        jax.ShapeDtypeStruct((s, width), BF16),
        tm * tn * 2, [pltpu.VMEM((SUBLANES, tn), F32)], tm, tn)


def _softplus2(z2):
    return jnp.maximum(z2, 0.0) + jnp.log(1.0 + jnp.exp2(-jnp.abs(z2))) * LOG2_E


def _split_hi_lo(x):
    hi = x.astype(BF16)
    lo = (x - hi.astype(F32)).astype(BF16)
    return jnp.concatenate([hi, lo], axis=1)


def _attn_kernel(q_ref, k_ref, v_ref, sg_ref, o_ref, mm_ref):
    n_heads, sq, dh = q_ref.shape
    nqb = sq // QB
    first_block = pl.program_id(1) * nqb
    scale2 = LOG2_E / math.sqrt(HEAD_DIM)
    nt = (((1,), (1,)), ((), ()))

    r = lax.broadcasted_iota(jnp.int32, (2 * KB, 2 * KB), 0) & (KB - 1)
    cc = lax.broadcasted_iota(jnp.int32, (2 * KB, 2 * KB), 1)
    mm_ref[...] = jnp.where((cc >= KB) | (r >= cc), -1.0, 0.0).astype(BF16)

    def scores(q, kblk):
        return lax.dot_general(q, kblk, nt, preferred_element_type=F32) * scale2

    def causal_mask():
        row = lax.broadcasted_iota(jnp.int32, (QB, KB), 0)
        col = lax.broadcasted_iota(jnp.int32, (QB, KB), 1)
        return col < row

    def windows(chains, tri, nblk):
        wss = [pl.multiple_of((ib - (nblk - 1)) * KB, KB) for _, _, ib in chains]
        zs = [scores(q_ref[g, pl.ds(r0, QB), :], k_ref[g, pl.ds(ws, nblk * KB), :])
              for (g, r0, _), ws in zip(chains, wss)]
        lhs = []
        for z in zs:
            sp = _softplus2(z)
            sps = [sp[:, b * KB:(b + 1) * KB] for b in range(nblk)]
            sps[-1] = jnp.where(tri, sps[-1], 0.0)
            lhs.append(jnp.concatenate([_split_hi_lo(s) for s in sps], axis=0))
        ps = [jnp.dot(l, mm_ref[...], preferred_element_type=F32) for l in lhs]
        a_all, carries = [], []
        for z, p in zip(zs, ps):
            carry = None
            a_blocks = [None] * nblk
            for b in reversed(range(nblk)):
                pb = p[b * QB:(b + 1) * QB]
                log2_a = z[:, b * KB:(b + 1) * KB] + pb[:, :KB]
                if carry is not None:
                    log2_a = log2_a + carry
                a = jnp.exp2(log2_a)
                if b == nblk - 1:
                    a = jnp.where(tri, a, 0.0)
                a_blocks[b] = a.astype(BF16)
                carry = pb[:, KB:] if carry is None else carry + pb[:, KB:]
            a_all.append(jnp.concatenate(a_blocks, axis=1))
            carries.append(carry)
        accs = [jnp.dot(a, v_ref[g, pl.ds(ws, nblk * KB), :], preferred_element_type=F32)
                for a, (g, _, _), ws in zip(a_all, chains, wss)]
        return list(zip(carries, accs))

    def sweep_rest(g, r0, kb, carry, acc):
        q = q_ref[g, pl.ds(r0, QB), :]

        def cond(st):
            kb, _, _, mx = st
            return jnp.logical_and(kb >= 0, mx > LOG2_ZERO_F32)

        def body(st):
            kb, carry, acc, _ = st
            k0 = pl.multiple_of(kb * KB, KB)
            z = scores(q, k_ref[g, pl.ds(k0, KB), :])
            p = jnp.dot(_split_hi_lo(_softplus2(z)), mm_ref[...], preferred_element_type=F32)
            a = jnp.exp2(z + p[:, :KB] + carry)
            acc = acc + jnp.dot(a.astype(BF16), v_ref[g, pl.ds(k0, KB), :], preferred_element_type=F32)
            carry = carry + p[:, KB:]
            return kb - 1, carry, acc, jnp.max(carry)

        _, _, acc, _ = lax.while_loop(cond, body, (kb, carry, acc, jnp.max(carry)))
        return acc

    def qblocks(ils, nblk):
        chains = []
        for il in ils:
            r0 = il * QB if isinstance(il, int) else pl.multiple_of(il * QB, QB)
            chains += [(g, r0, first_block + il) for g in range(n_heads)]
        res = windows(chains, causal_mask(), nblk)
        live = jnp.max(functools.reduce(jnp.maximum, [c for c, _ in res])) > LOG2_ZERO_F32
        accs = lax.cond(
            live,
            lambda: [sweep_rest(g, r0, ib - nblk, *rs) for (g, r0, ib), rs in zip(chains, res)],
            lambda: [a for _, a in res])
        for (g, r0, _), acc in zip(chains, accs):
            gated = acc * sg_ref[g, pl.ds(r0, QB), :]
            o_ref[pl.ds(r0, QB), g * dh:(g + 1) * dh] = gated.astype(o_ref.dtype)

    @pl.when(first_block == 0)
    def _():
        for il in range(WIN - 1):
            qblocks([il], il + 1)

    def body(it, _):
        qblocks([it * Q_BLOCKS_PER_ITER + j for j in range(Q_BLOCKS_PER_ITER)], WIN)
        return 0

    assert (WIN - 1) % Q_BLOCKS_PER_ITER == 0 and nqb % Q_BLOCKS_PER_ITER == 0
    lax.fori_loop(jnp.where(first_block == 0, (WIN - 1) // Q_BLOCKS_PER_ITER, 0),
                  nqb // Q_BLOCKS_PER_ITER, body, 0)


def _attention(q, k, v, sg, heads=HEADS_PER_STEP, sq=Q_ROWS_PER_STEP):
    h, s, dh = q.shape
    q_spec = pl.BlockSpec((heads, sq, dh), lambda hg, j: (hg, j, 0))
    kv_spec = pl.BlockSpec((heads, s, dh), lambda hg, j: (hg, 0, 0))
    block_bytes = heads * dh * (sq * (2 + 4 + 2) + 2 * s * 2)
    return pl.pallas_call(
        _attn_kernel,
        grid=(h // heads, s // sq),
        in_specs=[q_spec, kv_spec, kv_spec, q_spec],
        out_specs=pl.BlockSpec((sq, heads * dh), lambda hg, j: (j, hg)),
        out_shape=jax.ShapeDtypeStruct((s, h * dh), BF16),
        scratch_shapes=[pltpu.VMEM((2 * KB, 2 * KB), BF16)],
        compiler_params=pltpu.CompilerParams(
            dimension_semantics=("arbitrary", "arbitrary"),
            vmem_limit_bytes=_vmem_limit(block_bytes, 4 * KB * KB * 2, 8 << 20)),
        name="stickbreak_attn",
    )(q, k, v, sg)


def _outproj_kernel(a_ref, w_ref, res_ref, o_ref):
    o_ref[...] = res_ref[...] + jnp.dot(a_ref[...], w_ref[...].astype(BF16), preferred_element_type=F32)


def _outproj_residual(a, w, res, tm=1024, tn=512):
    s, kdim = a.shape
    n = w.shape[1]
    block_bytes = tm * kdim * 2 + kdim * tn * 4 + 2 * tm * tn * 4
    return pl.pallas_call(
        _outproj_kernel,
        grid=(s // tm, n // tn),
        in_specs=[pl.BlockSpec((tm, kdim), lambda m, j: (m, 0)),
                  pl.BlockSpec((kdim, tn), lambda m, j: (0, j)),
                  pl.BlockSpec((tm, tn), lambda m, j: (m, j))],
        out_specs=pl.BlockSpec((tm, tn), lambda m, j: (m, j)),
        out_shape=jax.ShapeDtypeStruct((s, n), F32),
        compiler_params=pltpu.CompilerParams(
            dimension_semantics=("arbitrary", "arbitrary"),
            vmem_limit_bytes=_vmem_limit(block_bytes, 0, 2 * tm * tn * 4 + kdim * tn * 2)),
        name="attn_outproj",
    )(a, w, res)


def _outproj_norm_kernel(a_ref, w_ref, res_ref, gain_ref, o_ref):
    kk = pl.program_id(1)

    @pl.when(kk == 0)
    def _():
        o_ref[...] = res_ref[...]

    o_ref[...] += jnp.dot(a_ref[...], w_ref[...], preferred_element_type=F32)

    @pl.when(kk == pl.num_programs(1) - 1)
    def _():
        _rmsnorm_rows(o_ref, gain_ref, o_ref, o_ref.shape[0])


def _outproj_residual_norm(a, w_bf16, res, gain, tm=512, tk=1024):
    s, kdim = a.shape
    n = w_bf16.shape[1]
    block_bytes = tm * tk * 2 + tk * n * 2 + 2 * tm * n * 4 + n * 4
    return pl.pallas_call(
        _outproj_norm_kernel,
        grid=(s // tm, kdim // tk),
        in_specs=[pl.BlockSpec((tm, tk), lambda m, kk: (m, kk)),
                  pl.BlockSpec((tk, n), lambda m, kk: (kk, 0)),
                  pl.BlockSpec((tm, n), lambda m, kk: (m, 0)),
                  pl.BlockSpec((1, n), lambda m, kk: (0, 0))],
        out_specs=pl.BlockSpec((tm, n), lambda m, kk: (m, 0)),
        out_shape=jax.ShapeDtypeStruct((s, n), F32),
        compiler_params=pltpu.CompilerParams(
            dimension_semantics=("arbitrary", "arbitrary"),
            vmem_limit_bytes=_vmem_limit(block_bytes, 0, tm * n * 4)),
        name="conv_outproj_norm",
    )(a, w_bf16, res, gain)


def kernel(x, norm_attn, w_in_attn, w_out_attn, norm_conv, w_in_conv, conv_w, w_out_conv, final_norm):
    b, s, d = x.shape
    assert b == 1 and d == N_HEADS * HEAD_DIM
    x2 = x.reshape(s, d)
    row = lambda g: g.reshape(1, d).astype(F32)

    q, k, v, sg = _attn_inproj(_rmsnorm_bf16(x2, row(norm_attn)), w_in_attn)
    og = _attention(q, k, v, sg)
    h1 = _outproj_residual(og, w_out_attn, x2)
    yg = _conv_inproj(_rmsnorm_bf16(h1, row(norm_conv)), w_in_conv, conv_w.astype(F32))
    out = _outproj_residual_norm(yg, w_out_conv.astype(BF16), h1, row(final_norm))
    return out.reshape(b, s, d)
```

```python
import functools
import math

import jax
import jax.numpy as jnp
from jax import lax
from jax.experimental import pallas as pl
from jax.experimental.pallas import tpu as pltpu

F32 = jnp.float32
BF16 = jnp.bfloat16

RMS_EPS = 1e-6
N_HEADS = 32
HEAD_DIM = 128
CONV_K = 3

LANES = 128
SUBLANES = 8
VMEM_BYTES_V7X = 64 * 1024 * 1024

QB = 128
KB = 128
WIN = 3
HEADS_PER_STEP = 4
Q_ROWS_PER_STEP = 1024
Q_BLOCKS_PER_ITER = 2
LOG2_E = math.log2(math.e)
SCORE_SCALE2 = LOG2_E / math.sqrt(HEAD_DIM)
LOG2_ZERO_F32 = -152.0

NORM_ROWS = 16
NORM_UNROLL = 4


def _vmem_limit(block_bytes, scratch_bytes, temp_bytes):
    need = 2 * block_bytes + scratch_bytes + temp_bytes + (4 << 20)
    return int(min(need, VMEM_BYTES_V7X - (4 << 20)))


def _silu(g):
    return g * (1.0 / (1.0 + jnp.exp(-g)))


def _rmsnorm_rows(x_ref, gain_ref, dst_ref, rows):
    group = NORM_ROWS * NORM_UNROLL

    def body(r, _):
        base = r * group
        sls = [pl.ds(pl.multiple_of(base + j * NORM_ROWS, NORM_ROWS), NORM_ROWS) for j in range(NORM_UNROLL)]
        inv = []
        for sl in sls:
            xf = x_ref[sl, :]
            inv.append(lax.rsqrt(jnp.mean(xf * xf, axis=-1, keepdims=True) + RMS_EPS))
        for sl, rs in zip(sls, inv):
            dst_ref[sl, :] = (x_ref[sl, :] * rs * gain_ref[...]).astype(dst_ref.dtype)
        return 0
    lax.fori_loop(0, rows // group, body, 0)


def _norm_kernel(x_ref, gain_ref, o_ref):
    _rmsnorm_rows(x_ref, gain_ref, o_ref, x_ref.shape[0])


def _rmsnorm_bf16(x, gain, tm=512):
    s, d = x.shape
    return pl.pallas_call(
        _norm_kernel,
        grid=(s // tm,),
        in_specs=[pl.BlockSpec((tm, d), lambda m: (m, 0)), pl.BlockSpec((1, d), lambda m: (0, 0))],
        out_specs=pl.BlockSpec((tm, d), lambda m: (m, 0)),
        out_shape=jax.ShapeDtypeStruct((s, d), BF16),
        compiler_params=pltpu.CompilerParams(
            dimension_semantics=("arbitrary",),
            vmem_limit_bytes=_vmem_limit(tm * d * (4 + 2) + d * 4, 0, 4 << 20)),
        name="rmsnorm_bf16",
    )(x, gain)


N_PROJ = 4
CAST_ROWS = 256


def _stage_weights(w_hbm, stage_ref, wbf_ref, sem, c, nblk):
    tn = stage_ref.shape[2]
    width = nblk * tn

    def copy(j, cc):
        col = pl.multiple_of(j * width + cc * tn, tn)
        return pltpu.make_async_copy(w_hbm.at[:, pl.ds(col, tn)], stage_ref.at[j], sem.at[j])

    @pl.when(c == 0)
    def _():
        for j in range(N_PROJ):
            copy(j, c).start()

    for j in range(N_PROJ):
        copy(j, c).wait()

    def cast(r, _):
        rows = pl.ds(pl.multiple_of(r * CAST_ROWS, CAST_ROWS), CAST_ROWS)
        for j in range(N_PROJ):
            wbf_ref[j, rows, :] = stage_ref[j, rows, :].astype(BF16)
        return 0
    lax.fori_loop(0, stage_ref.shape[1] // CAST_ROWS, cast, 0)

    @pl.when(c + 1 < nblk)
    def _():
        for j in range(N_PROJ):
            copy(j, c + 1).start()


def _proj(xn_ref, wbf_ref, j):
    return jnp.dot(xn_ref[...], wbf_ref[j], preferred_element_type=F32)


def _row_rsqrt(ssq_ref, d):
    return lax.rsqrt(ssq_ref[:, 0:1] * (1.0 / d) + RMS_EPS)


def _attn_inproj_kernel(xn_ref, w_hbm, q_ref, k_ref, v_ref, sg_ref, stage_ref, wbf_ref, sem):
    @pl.when(pl.program_id(1) == 0)
    def _():
        _stage_weights(w_hbm, stage_ref, wbf_ref, sem, pl.program_id(0), pl.num_programs(0))

    heads = [slice(hh * HEAD_DIM, (hh + 1) * HEAD_DIM) for hh in range(q_ref.shape[0])]
    sg = _silu(_proj(xn_ref, wbf_ref, 3))
    for hh, cols in enumerate(heads):
        sg_ref[hh] = sg[:, cols]
    for j, (o_ref, mult) in enumerate(((q_ref, SCORE_SCALE2), (k_ref, None), (v_ref, None))):
        p = _proj(xn_ref, wbf_ref, j)
        if mult is not None:
            p = p * mult
        for hh, cols in enumerate(heads):
            o_ref[hh] = p[:, cols].astype(BF16)


def _conv_inproj_kernel(hg_ref, ssq_ref, w_hbm, cw_ref, y_ref, stage_ref, wbf_ref, sem, halo_ref):
    tm, d = hg_ref.shape
    m = pl.program_id(1)
    xn_ref = hg_ref

    @pl.when(m == 0)
    def _():
        _stage_weights(w_hbm, stage_ref, wbf_ref, sem, pl.program_id(0), pl.num_programs(0))

    rs = _row_rsqrt(ssq_ref, d)
    sg = _silu(rs * _proj(xn_ref, wbf_ref, 3))
    cu = (rs * _proj(xn_ref, wbf_ref, 1)) * (rs * _proj(xn_ref, wbf_ref, 2))

    halo = jnp.where(m > 0, halo_ref[...], 0.0)
    prev1 = halo[SUBLANES - 1:SUBLANES, :]
    prev2 = halo[SUBLANES - 2:SUBLANES - 1, :]
    row = lax.broadcasted_iota(jnp.int32, cu.shape, 0)
    cu1 = jnp.where(row == 0, prev1, pltpu.roll(cu, 1, 0))
    cu2 = jnp.where(row == 0, prev2, jnp.where(row == 1, prev1, pltpu.roll(cu, 2, 0)))
    halo_ref[...] = cu[tm - SUBLANES:, :]

    cw = cw_ref[...]
    conv = cw[0:1, :] * cu2 + cw[1:2, :] * cu1 + cw[2:3, :] * cu
    gb = rs * _proj(xn_ref, wbf_ref, 0)
    y_ref[...] = ((gb * conv) * sg).astype(y_ref.dtype)


def _inproj_call(kernel_fn, name, xn, w, extra_in, pre_in, pre_specs, extra_specs, out_specs, out_shape,
                 out_block_bytes, extra_scratch, tm, tn):
    s, d = xn.shape
    nblk = w.shape[1] // N_PROJ // tn
    x_spec = pl.BlockSpec((tm, d), lambda c, m: (m, 0))
    w_spec = pl.BlockSpec(memory_space=pl.ANY)
    scratch = [pltpu.VMEM((N_PROJ, d, tn), F32), pltpu.VMEM((N_PROJ, d, tn), BF16),
               pltpu.SemaphoreType.DMA((N_PROJ,))] + extra_scratch
    scratch_bytes = N_PROJ * d * tn * (4 + 2)
    temp_bytes = 8 * tm * tn * 4
    return pl.pallas_call(
        kernel_fn,
        grid=(nblk, s // tm),
        in_specs=[x_spec] + pre_specs + [w_spec] + extra_specs,
        out_specs=out_specs,
        out_shape=out_shape,
        scratch_shapes=scratch,
        compiler_params=pltpu.CompilerParams(
            dimension_semantics=("arbitrary", "arbitrary"),
            vmem_limit_bytes=_vmem_limit(tm * d * 2 + out_block_bytes, scratch_bytes, temp_bytes)),
        name=name,
    )(xn, *pre_in, w, *extra_in)


def _attn_inproj(xn, w, tm=1024, tn=256):
    s = xn.shape[0]
    hpt = tn // HEAD_DIM
    head_spec = pl.BlockSpec((hpt, tm, HEAD_DIM), lambda c, m: (c, m, 0))
    return _inproj_call(
        _attn_inproj_kernel, "attn_inproj", xn, w, [], [], [], [],
        [head_spec] * 4,
        [jax.ShapeDtypeStruct((N_HEADS, s, HEAD_DIM), BF16)] * 3
        + [jax.ShapeDtypeStruct((N_HEADS, s, HEAD_DIM), F32)],
        hpt * tm * HEAD_DIM * (3 * 2 + 4), [], tm, tn)


def _conv_inproj(hg, ssq, w, conv_w, tm=1024, tn=256):
    s = hg.shape[0]
    width = w.shape[1] // N_PROJ
    return _inproj_call(
        _conv_inproj_kernel, "conv_inproj", hg, w, [conv_w], [ssq],
        [pl.BlockSpec((tm, LANES), lambda c, m: (m, 0))],
        [pl.BlockSpec((CONV_K, tn), lambda c, m: (0, c))],
        pl.BlockSpec((tm, tn), lambda c, m: (m, c)),
        jax.ShapeDtypeStruct((s, width), BF16),
        tm * tn * 2, [pltpu.VMEM((SUBLANES, tn), F32)], tm, tn)


def _softplus2(z2):
    return jnp.maximum(z2, 0.0) + jnp.log(1.0 + jnp.exp2(-jnp.abs(z2))) * LOG2_E


def _split_hi_lo(x):
    hi = x.astype(BF16)
    lo = (x - hi.astype(F32)).astype(BF16)
    return jnp.concatenate([hi, lo], axis=1)


def _attn_kernel(q_ref, k_ref, v_ref, sg_ref, o_ref, mm_ref):
    n_heads, sq, dh = q_ref.shape
    nqb = sq // QB
    first_block = pl.program_id(1) * nqb
    nt = (((1,), (1,)), ((), ()))

    r = lax.broadcasted_iota(jnp.int32, (2 * KB, 2 * KB), 0) & (KB - 1)
    cc = lax.broadcasted_iota(jnp.int32, (2 * KB, 2 * KB), 1)
    mm_ref[...] = jnp.where((cc >= KB) | (r >= cc), -1.0, 0.0).astype(BF16)

    def scores(q, kblk):
        return lax.dot_general(q, kblk, nt, preferred_element_type=F32)

    def causal_mask():
        row = lax.broadcasted_iota(jnp.int32, (QB, KB), 0)
        col = lax.broadcasted_iota(jnp.int32, (QB, KB), 1)
        return col < row

    def windows(chains, tri, nblk):
        wss = [pl.multiple_of((ib - (nblk - 1)) * KB, KB) for _, _, ib in chains]
        zs = [scores(q_ref[g, pl.ds(r0, QB), :], k_ref[g, pl.ds(ws, nblk * KB), :])
              for (g, r0, _), ws in zip(chains, wss)]
        lhs = []
        for z in zs:
            sp = _softplus2(z)
            sps = [sp[:, b * KB:(b + 1) * KB] for b in range(nblk)]
            sps[-1] = jnp.where(tri, sps[-1], 0.0)
            lhs.append(jnp.concatenate([_split_hi_lo(s) for s in sps], axis=0))
        ps = [jnp.dot(l, mm_ref[...], preferred_element_type=F32) for l in lhs]
        a_all, carries = [], []
        for z, p in zip(zs, ps):
            carry = None
            a_blocks = [None] * nblk
            for b in reversed(range(nblk)):
                pb = p[b * QB:(b + 1) * QB]
                log2_a = z[:, b * KB:(b + 1) * KB] + pb[:, :KB]
                if carry is not None:
                    log2_a = log2_a + carry
                a = jnp.exp2(log2_a)
                if b == nblk - 1:
                    a = jnp.where(tri, a, 0.0)
                a_blocks[b] = a.astype(BF16)
                carry = pb[:, KB:] if carry is None else carry + pb[:, KB:]
            a_all.append(jnp.concatenate(a_blocks, axis=1))
            carries.append(carry)
        accs = [jnp.dot(a, v_ref[g, pl.ds(ws, nblk * KB), :], preferred_element_type=F32)
                for a, (g, _, _), ws in zip(a_all, chains, wss)]
        return list(zip(carries, accs))

    def sweep_rest(g, r0, kb, carry, acc):
        q = q_ref[g, pl.ds(r0, QB), :]

        def cond(st):
            kb, _, _, mx = st
            return jnp.logical_and(kb >= 0, mx > LOG2_ZERO_F32)

        def body(st):
            kb, carry, acc, _ = st
            k0 = pl.multiple_of(kb * KB, KB)
            z = scores(q, k_ref[g, pl.ds(k0, KB), :])
            p = jnp.dot(_split_hi_lo(_softplus2(z)), mm_ref[...], preferred_element_type=F32)
            a = jnp.exp2(z + p[:, :KB] + carry)
            acc = acc + jnp.dot(a.astype(BF16), v_ref[g, pl.ds(k0, KB), :], preferred_element_type=F32)
            carry = carry + p[:, KB:]
            return kb - 1, carry, acc, jnp.max(carry)

        _, _, acc, _ = lax.while_loop(cond, body, (kb, carry, acc, jnp.max(carry)))
        return acc

    def qblocks(ils, nblk):
        chains = []
        for il in ils:
            r0 = il * QB if isinstance(il, int) else pl.multiple_of(il * QB, QB)
            chains += [(g, r0, first_block + il) for g in range(n_heads)]
        res = windows(chains, causal_mask(), nblk)
        live = jnp.max(functools.reduce(jnp.maximum, [c for c, _ in res])) > LOG2_ZERO_F32
        accs = lax.cond(
            live,
            lambda: [sweep_rest(g, r0, ib - nblk, *rs) for (g, r0, ib), rs in zip(chains, res)],
            lambda: [a for _, a in res])
        for (g, r0, _), acc in zip(chains, accs):
            gated = acc * sg_ref[g, pl.ds(r0, QB), :]
            o_ref[pl.ds(r0, QB), g * dh:(g + 1) * dh] = gated.astype(o_ref.dtype)

    @pl.when(first_block == 0)
    def _():
        for il in range(WIN - 1):
            qblocks([il], il + 1)

    def body(it, _):
        qblocks([it * Q_BLOCKS_PER_ITER + j for j in range(Q_BLOCKS_PER_ITER)], WIN)
        return 0

    assert (WIN - 1) % Q_BLOCKS_PER_ITER == 0 and nqb % Q_BLOCKS_PER_ITER == 0
    lax.fori_loop(jnp.where(first_block == 0, (WIN - 1) // Q_BLOCKS_PER_ITER, 0),
                  nqb // Q_BLOCKS_PER_ITER, body, 0)


def _attention(q, k, v, sg, heads=HEADS_PER_STEP, sq=Q_ROWS_PER_STEP):
    h, s, dh = q.shape
    q_spec = pl.BlockSpec((heads, sq, dh), lambda hg, j: (hg, j, 0))
    kv_spec = pl.BlockSpec((heads, s, dh), lambda hg, j: (hg, 0, 0))
    block_bytes = heads * dh * (sq * (2 + 4 + 2) + 2 * s * 2)
    return pl.pallas_call(
        _attn_kernel,
        grid=(h // heads, s // sq),
        in_specs=[q_spec, kv_spec, kv_spec, q_spec],
        out_specs=pl.BlockSpec((sq, heads * dh), lambda hg, j: (j, hg)),
        out_shape=jax.ShapeDtypeStruct((s, h * dh), BF16),
        scratch_shapes=[pltpu.VMEM((2 * KB, 2 * KB), BF16)],
        compiler_params=pltpu.CompilerParams(
            dimension_semantics=("arbitrary", "arbitrary"),
            vmem_limit_bytes=_vmem_limit(block_bytes, 4 * KB * KB * 2, 8 << 20)),
        name="stickbreak_attn",
    )(q, k, v, sg)


def _outproj_kernel(a_ref, w_ref, res_ref, gain_ref, h_ref, hg_ref, ssq_ref):
    h = res_ref[...] + jnp.dot(a_ref[...], w_ref[...].astype(BF16), preferred_element_type=F32)
    h_ref[...] = h
    hg_ref[...] = (h * gain_ref[...]).astype(hg_ref.dtype)
    part = jnp.broadcast_to(jnp.sum(h * h, axis=1, keepdims=True), ssq_ref.shape)

    @pl.when(pl.program_id(1) == 0)
    def _():
        ssq_ref[...] = part

    @pl.when(pl.program_id(1) > 0)
    def _():
        ssq_ref[...] += part


def _outproj_residual(a, w, res, gain, tm=1024, tn=512):
    s, kdim = a.shape
    n = w.shape[1]
    block_bytes = tm * kdim * 2 + kdim * tn * 4 + tm * tn * (4 + 4 + 2) + tn * 4 + tm * LANES * 4
    tile = pl.BlockSpec((tm, tn), lambda m, j: (m, j))
    return pl.pallas_call(
        _outproj_kernel,
        grid=(s // tm, n // tn),
        in_specs=[pl.BlockSpec((tm, kdim), lambda m, j: (m, 0)),
                  pl.BlockSpec((kdim, tn), lambda m, j: (0, j)),
                  tile,
                  pl.BlockSpec((1, tn), lambda m, j: (0, j))],
        out_specs=[tile, tile, pl.BlockSpec((tm, LANES), lambda m, j: (m, 0))],
        out_shape=[jax.ShapeDtypeStruct((s, n), F32), jax.ShapeDtypeStruct((s, n), BF16),
                   jax.ShapeDtypeStruct((s, LANES), F32)],
        compiler_params=pltpu.CompilerParams(
            dimension_semantics=("arbitrary", "arbitrary"),
            vmem_limit_bytes=_vmem_limit(block_bytes, 0, 3 * tm * tn * 4 + kdim * tn * 2)),
        name="attn_outproj",
    )(a, w, res, gain)


def _outproj_norm_kernel(a_ref, w_ref, res_ref, gain_ref, o_ref):
    kk = pl.program_id(1)

    @pl.when(kk == 0)
    def _():
        o_ref[...] = res_ref[...]

    o_ref[...] += jnp.dot(a_ref[...], w_ref[...], preferred_element_type=F32)

    @pl.when(kk == pl.num_programs(1) - 1)
    def _():
        _rmsnorm_rows(o_ref, gain_ref, o_ref, o_ref.shape[0])


def _outproj_residual_norm(a, w_bf16, res, gain, tm=512, tk=1024):
    s, kdim = a.shape
    n = w_bf16.shape[1]
    block_bytes = tm * tk * 2 + tk * n * 2 + 2 * tm * n * 4 + n * 4
    return pl.pallas_call(
        _outproj_norm_kernel,
        grid=(s // tm, kdim // tk),
        in_specs=[pl.BlockSpec((tm, tk), lambda m, kk: (m, kk)),
                  pl.BlockSpec((tk, n), lambda m, kk: (kk, 0)),
                  pl.BlockSpec((tm, n), lambda m, kk: (m, 0)),
                  pl.BlockSpec((1, n), lambda m, kk: (0, 0))],
        out_specs=pl.BlockSpec((tm, n), lambda m, kk: (m, 0)),
        out_shape=jax.ShapeDtypeStruct((s, n), F32),
        compiler_params=pltpu.CompilerParams(
            dimension_semantics=("arbitrary", "arbitrary"),
            vmem_limit_bytes=_vmem_limit(block_bytes, 0, tm * n * 4)),
        name="conv_outproj_norm",
    )(a, w_bf16, res, gain)


def kernel(x, norm_attn, w_in_attn, w_out_attn, norm_conv, w_in_conv, conv_w, w_out_conv, final_norm):
    b, s, d = x.shape
    assert b == 1 and d == N_HEADS * HEAD_DIM
    x2 = x.reshape(s, d)
    row = lambda g: g.reshape(1, d).astype(F32)

    q, k, v, sg = _attn_inproj(_rmsnorm_bf16(x2, row(norm_attn)), w_in_attn)
    og = _attention(q, k, v, sg)
    h1, h1g, ssq = _outproj_residual(og, w_out_attn, x2, row(norm_conv))
    yg = _conv_inproj(h1g, ssq, w_in_conv, conv_w.astype(F32))
    out = _outproj_residual_norm(yg, w_out_conv.astype(BF16), h1, row(final_norm))
    return out.reshape(b, s, d)
```

```python
import functools
import math

import jax
import jax.numpy as jnp
from jax import lax
from jax.experimental import pallas as pl
from jax.experimental.pallas import tpu as pltpu

F32 = jnp.float32
BF16 = jnp.bfloat16

RMS_EPS = 1e-6
N_HEADS = 32
HEAD_DIM = 128
CONV_K = 3

LANES = 128
SUBLANES = 8
VMEM_BYTES_V7X = 64 * 1024 * 1024

QB = 128
KB = 128
WIN = 3
HEADS_PER_STEP = 4
Q_ROWS_PER_STEP = 2048
Q_BLOCKS_PER_ITER = 2
LOG2_E = math.log2(math.e)
SCORE_SCALE2 = LOG2_E / math.sqrt(HEAD_DIM)
LOG2_ZERO_F32 = -152.0

NORM_ROWS = 16
NORM_UNROLL = 4


def _vmem_limit(block_bytes, scratch_bytes, temp_bytes):
    need = 2 * block_bytes + scratch_bytes + temp_bytes + (4 << 20)
    return int(min(need, VMEM_BYTES_V7X - (4 << 20)))


def _silu(g):
    return g * (1.0 / (1.0 + jnp.exp(-g)))


def _rmsnorm_rows(x_ref, gain_ref, dst_ref, rows):
    group = NORM_ROWS * NORM_UNROLL

    def body(r, _):
        base = r * group
        sls = [pl.ds(pl.multiple_of(base + j * NORM_ROWS, NORM_ROWS), NORM_ROWS) for j in range(NORM_UNROLL)]
        inv = []
        for sl in sls:
            xf = x_ref[sl, :]
            inv.append(lax.rsqrt(jnp.mean(xf * xf, axis=-1, keepdims=True) + RMS_EPS))
        for sl, rs in zip(sls, inv):
            dst_ref[sl, :] = (x_ref[sl, :] * rs * gain_ref[...]).astype(dst_ref.dtype)
        return 0
    lax.fori_loop(0, rows // group, body, 0)


def _norm_kernel(x_ref, gain_ref, o_ref):
    _rmsnorm_rows(x_ref, gain_ref, o_ref, x_ref.shape[0])


def _rmsnorm_bf16(x, gain, tm=512):
    s, d = x.shape
    return pl.pallas_call(
        _norm_kernel,
        grid=(s // tm,),
        in_specs=[pl.BlockSpec((tm, d), lambda m: (m, 0)), pl.BlockSpec((1, d), lambda m: (0, 0))],
        out_specs=pl.BlockSpec((tm, d), lambda m: (m, 0)),
        out_shape=jax.ShapeDtypeStruct((s, d), BF16),
        compiler_params=pltpu.CompilerParams(
            dimension_semantics=("arbitrary",),
            vmem_limit_bytes=_vmem_limit(tm * d * (4 + 2) + d * 4, 0, 4 << 20)),
        name="rmsnorm_bf16",
    )(x, gain)


N_PROJ = 4
CAST_ROWS = 256


def _stage_weights(w_hbm, stage_ref, wbf_ref, sem, c, nblk):
    tn = stage_ref.shape[2]
    width = nblk * tn

    def copy(j, cc):
        col = pl.multiple_of(j * width + cc * tn, tn)
        return pltpu.make_async_copy(w_hbm.at[:, pl.ds(col, tn)], stage_ref.at[j], sem.at[j])

    @pl.when(c == 0)
    def _():
        for j in range(N_PROJ):
            copy(j, c).start()

    for j in range(N_PROJ):
        copy(j, c).wait()

    def cast(r, _):
        rows = pl.ds(pl.multiple_of(r * CAST_ROWS, CAST_ROWS), CAST_ROWS)
        for j in range(N_PROJ):
            wbf_ref[j, rows, :] = stage_ref[j, rows, :].astype(BF16)
        return 0
    lax.fori_loop(0, stage_ref.shape[1] // CAST_ROWS, cast, 0)

    @pl.when(c + 1 < nblk)
    def _():
        for j in range(N_PROJ):
            copy(j, c + 1).start()


def _proj(xn_ref, wbf_ref, j):
    return jnp.dot(xn_ref[...], wbf_ref[j], preferred_element_type=F32)


def _row_rsqrt(ssq_ref, d):
    return lax.rsqrt(jnp.sum(ssq_ref[...], axis=1, keepdims=True) * (1.0 / d) + RMS_EPS)


def _attn_inproj_kernel(xn_ref, w_hbm, wo_attn_ref, wo_conv_ref, q_ref, k_ref, v_ref, sg_ref,
                        wo_attn_bf_ref, wo_conv_bf_ref, stage_ref, wbf_ref, sem):
    @pl.when(pl.program_id(1) == 0)
    def _():
        _stage_weights(w_hbm, stage_ref, wbf_ref, sem, pl.program_id(0), pl.num_programs(0))

    wo_attn_bf_ref[...] = wo_attn_ref[...].astype(BF16)
    wo_conv_bf_ref[...] = wo_conv_ref[...].astype(BF16)

    heads = [slice(hh * HEAD_DIM, (hh + 1) * HEAD_DIM) for hh in range(q_ref.shape[0])]
    sg = _silu(_proj(xn_ref, wbf_ref, 3))
    for hh, cols in enumerate(heads):
        sg_ref[hh] = sg[:, cols]
    for j, (o_ref, mult) in enumerate(((q_ref, SCORE_SCALE2), (k_ref, None), (v_ref, None))):
        p = _proj(xn_ref, wbf_ref, j)
        if mult is not None:
            p = p * mult
        for hh, cols in enumerate(heads):
            o_ref[hh] = p[:, cols].astype(BF16)


def _conv_inproj_kernel(hg_ref, ssq_ref, w_hbm, cw_ref, y_ref, stage_ref, wbf_ref, sem, halo_ref):
    tm, d = hg_ref.shape
    m = pl.program_id(1)
    xn_ref = hg_ref

    @pl.when(m == 0)
    def _():
        _stage_weights(w_hbm, stage_ref, wbf_ref, sem, pl.program_id(0), pl.num_programs(0))

    rs = _row_rsqrt(ssq_ref, d)
    sg = _silu(rs * _proj(xn_ref, wbf_ref, 3))
    cu = (rs * _proj(xn_ref, wbf_ref, 1)) * (rs * _proj(xn_ref, wbf_ref, 2))

    halo = jnp.where(m > 0, halo_ref[...], 0.0)
    prev1 = halo[SUBLANES - 1:SUBLANES, :]
    prev2 = halo[SUBLANES - 2:SUBLANES - 1, :]
    row = lax.broadcasted_iota(jnp.int32, cu.shape, 0)
    cu1 = jnp.where(row == 0, prev1, pltpu.roll(cu, 1, 0))
    cu2 = jnp.where(row == 0, prev2, jnp.where(row == 1, prev1, pltpu.roll(cu, 2, 0)))
    halo_ref[...] = cu[tm - SUBLANES:, :]

    cw = cw_ref[...]
    conv = cw[0:1, :] * cu2 + cw[1:2, :] * cu1 + cw[2:3, :] * cu
    gb = rs * _proj(xn_ref, wbf_ref, 0)
    y_ref[...] = ((gb * conv) * sg).astype(y_ref.dtype)


def _inproj_call(kernel_fn, name, xn, w, extra_in, pre_in, pre_specs, extra_specs, out_specs, out_shape,
                 out_block_bytes, extra_scratch, tm, tn):
    s, d = xn.shape
    nblk = w.shape[1] // N_PROJ // tn
    x_spec = pl.BlockSpec((tm, d), lambda c, m: (m, 0))
    w_spec = pl.BlockSpec(memory_space=pl.ANY)
    scratch = [pltpu.VMEM((N_PROJ, d, tn), F32), pltpu.VMEM((N_PROJ, d, tn), BF16),
               pltpu.SemaphoreType.DMA((N_PROJ,))] + extra_scratch
    scratch_bytes = N_PROJ * d * tn * (4 + 2)
    temp_bytes = 8 * tm * tn * 4
    return pl.pallas_call(
        kernel_fn,
        grid=(nblk, s // tm),
        in_specs=[x_spec] + pre_specs + [w_spec] + extra_specs,
        out_specs=out_specs,
        out_shape=out_shape,
        scratch_shapes=scratch,
        compiler_params=pltpu.CompilerParams(
            dimension_semantics=("arbitrary", "arbitrary"),
            vmem_limit_bytes=_vmem_limit(tm * d * 2 + out_block_bytes, scratch_bytes, temp_bytes)),
        name=name,
    )(xn, *pre_in, w, *extra_in)


def _attn_inproj(xn, w, w_out_attn, w_out_conv, tm=1024, tn=256):
    s = xn.shape[0]
    hpt = tn // HEAD_DIM
    head_spec = pl.BlockSpec((hpt, tm, HEAD_DIM), lambda c, m: (c, m, 0))
    nm = s // tm
    kdim, n = w_out_attn.shape
    assert w_out_conv.shape == (kdim, n)
    slab = kdim // ((w.shape[1] // N_PROJ // tn) * nm)
    slab_spec = pl.BlockSpec((slab, n), lambda c, m: (c * nm + m, 0))
    return _inproj_call(
        _attn_inproj_kernel, "attn_inproj", xn, w, [w_out_attn, w_out_conv], [], [], [slab_spec] * 2,
        [head_spec] * 4 + [slab_spec] * 2,
        [jax.ShapeDtypeStruct((N_HEADS, s, HEAD_DIM), BF16)] * 3
        + [jax.ShapeDtypeStruct((N_HEADS, s, HEAD_DIM), F32)]
        + [jax.ShapeDtypeStruct((kdim, n), BF16)] * 2,
        hpt * tm * HEAD_DIM * (3 * 2 + 4) + 2 * slab * n * (4 + 2), [], tm, tn)


def _conv_inproj(hg, ssq, w, conv_w, tm=1024, tn=256):
    s = hg.shape[0]
    width = w.shape[1] // N_PROJ
    return _inproj_call(
        _conv_inproj_kernel, "conv_inproj", hg, w, [conv_w], [ssq],
        [pl.BlockSpec((tm, LANES), lambda c, m: (m, 0))],
        [pl.BlockSpec((CONV_K, tn), lambda c, m: (0, c))],
        pl.BlockSpec((tm, tn), lambda c, m: (m, c)),
        jax.ShapeDtypeStruct((s, width), BF16),
        tm * tn * 2, [pltpu.VMEM((SUBLANES, tn), F32)], tm, tn)


def _softplus2(z2):
    return jnp.maximum(z2, 0.0) + jnp.log(1.0 + jnp.exp2(-jnp.abs(z2))) * LOG2_E


def _split_hi_lo(x):
    hi = x.astype(BF16)
    lo = (x - hi.astype(F32)).astype(BF16)
    return jnp.concatenate([hi, lo], axis=1)


def _attn_kernel(q_ref, k_ref, v_ref, sg_ref, o_ref, mm_ref):
    n_heads, sq, dh = q_ref.shape
    nqb = sq // QB
    first_block = pl.program_id(1) * nqb
    nt = (((1,), (1,)), ((), ()))

    r = lax.broadcasted_iota(jnp.int32, (2 * KB, 2 * KB), 0) & (KB - 1)
    cc = lax.broadcasted_iota(jnp.int32, (2 * KB, 2 * KB), 1)
    mm_ref[...] = jnp.where((cc >= KB) | (r >= cc), -1.0, 0.0).astype(BF16)

    def scores(q, kblk):
        return lax.dot_general(q, kblk, nt, preferred_element_type=F32)

    def causal_mask():
        row = lax.broadcasted_iota(jnp.int32, (QB, KB), 0)
        col = lax.broadcasted_iota(jnp.int32, (QB, KB), 1)
        return col < row

    def windows(chains, tri, nblk):
        wss = [pl.multiple_of((ib - (nblk - 1)) * KB, KB) for _, _, ib in chains]
        zs = [scores(q_ref[g, pl.ds(r0, QB), :], k_ref[g, pl.ds(ws, nblk * KB), :])
              for (g, r0, _), ws in zip(chains, wss)]
        lhs = []
        for z in zs:
            sp = _softplus2(z)
            sps = [sp[:, b * KB:(b + 1) * KB] for b in range(nblk)]
            sps[-1] = jnp.where(tri, sps[-1], 0.0)
            lhs.append(jnp.concatenate([_split_hi_lo(s) for s in sps], axis=0))
        ps = [jnp.dot(l, mm_ref[...], preferred_element_type=F32) for l in lhs]
        a_all, carries = [], []
        for z, p in zip(zs, ps):
            carry = None
            a_blocks = [None] * nblk
            for b in reversed(range(nblk)):
                pb = p[b * QB:(b + 1) * QB]
                log2_a = z[:, b * KB:(b + 1) * KB] + pb[:, :KB]
                if carry is not None:
                    log2_a = log2_a + carry
                a = jnp.exp2(log2_a)
                if b == nblk - 1:
                    a = jnp.where(tri, a, 0.0)
                a_blocks[b] = a.astype(BF16)
                carry = pb[:, KB:] if carry is None else carry + pb[:, KB:]
            a_all.append(jnp.concatenate(a_blocks, axis=1))
            carries.append(carry)
        accs = [jnp.dot(a, v_ref[g, pl.ds(ws, nblk * KB), :], preferred_element_type=F32)
                for a, (g, _, _), ws in zip(a_all, chains, wss)]
        return list(zip(carries, accs))

    def sweep_rest(g, r0, kb, carry, acc):
        q = q_ref[g, pl.ds(r0, QB), :]

        def cond(st):
            kb, _, _, mx = st
            return jnp.logical_and(kb >= 0, mx > LOG2_ZERO_F32)

        def body(st):
            kb, carry, acc, _ = st
            k0 = pl.multiple_of(kb * KB, KB)
            z = scores(q, k_ref[g, pl.ds(k0, KB), :])
            p = jnp.dot(_split_hi_lo(_softplus2(z)), mm_ref[...], preferred_element_type=F32)
            a = jnp.exp2(z + p[:, :KB] + carry)
            acc = acc + jnp.dot(a.astype(BF16), v_ref[g, pl.ds(k0, KB), :], preferred_element_type=F32)
            carry = carry + p[:, KB:]
            return kb - 1, carry, acc, jnp.max(carry)

        _, _, acc, _ = lax.while_loop(cond, body, (kb, carry, acc, jnp.max(carry)))
        return acc

    def qblocks(ils, nblk):
        chains = []
        for il in ils:
            r0 = il * QB if isinstance(il, int) else pl.multiple_of(il * QB, QB)
            chains += [(g, r0, first_block + il) for g in range(n_heads)]
        res = windows(chains, causal_mask(), nblk)
        live = jnp.max(functools.reduce(jnp.maximum, [c for c, _ in res])) > LOG2_ZERO_F32
        accs = lax.cond(
            live,
            lambda: [sweep_rest(g, r0, ib - nblk, *rs) for (g, r0, ib), rs in zip(chains, res)],
            lambda: [a for _, a in res])
        for (g, r0, _), acc in zip(chains, accs):
            gated = acc * sg_ref[g, pl.ds(r0, QB), :]
            o_ref[pl.ds(r0, QB), g * dh:(g + 1) * dh] = gated.astype(o_ref.dtype)

    @pl.when(first_block == 0)
    def _():
        for il in range(WIN - 1):
            qblocks([il], il + 1)

    def body(it, _):
        qblocks([it * Q_BLOCKS_PER_ITER + j for j in range(Q_BLOCKS_PER_ITER)], WIN)
        return 0

    assert (WIN - 1) % Q_BLOCKS_PER_ITER == 0 and nqb % Q_BLOCKS_PER_ITER == 0
    lax.fori_loop(jnp.where(first_block == 0, (WIN - 1) // Q_BLOCKS_PER_ITER, 0),
                  nqb // Q_BLOCKS_PER_ITER, body, 0)


def _attention(q, k, v, sg, heads=HEADS_PER_STEP, sq=Q_ROWS_PER_STEP):
    h, s, dh = q.shape
    q_spec = pl.BlockSpec((heads, sq, dh), lambda hg, j: (hg, j, 0))
    kv_spec = pl.BlockSpec((heads, s, dh), lambda hg, j: (hg, 0, 0))
    block_bytes = heads * dh * (sq * (2 + 4 + 2) + 2 * s * 2)
    return pl.pallas_call(
        _attn_kernel,
        grid=(h // heads, s // sq),
        in_specs=[q_spec, kv_spec, kv_spec, q_spec],
        out_specs=pl.BlockSpec((sq, heads * dh), lambda hg, j: (j, hg)),
        out_shape=jax.ShapeDtypeStruct((s, h * dh), BF16),
        scratch_shapes=[pltpu.VMEM((2 * KB, 2 * KB), BF16)],
        compiler_params=pltpu.CompilerParams(
            dimension_semantics=("arbitrary", "arbitrary"),
            vmem_limit_bytes=_vmem_limit(block_bytes, 4 * KB * KB * 2, 8 << 20)),
        name="stickbreak_attn",
    )(q, k, v, sg)


def _outproj_kernel(a_ref, w_ref, res_ref, gain_ref, h_ref, hg_ref, ssq_ref):
    half = w_ref.shape[1] // 2
    part = None
    for j in range(2):
        cols = slice(j * half, (j + 1) * half)
        h = res_ref[:, cols] + jnp.dot(a_ref[...], w_ref[:, cols], preferred_element_type=F32)
        h_ref[:, cols] = h
        hg_ref[:, cols] = (h * gain_ref[:, cols]).astype(hg_ref.dtype)
        hh = h * h
        for g in range(half // LANES):
            lanes = hh[:, g * LANES:(g + 1) * LANES]
            part = lanes if part is None else part + lanes

    @pl.when(pl.program_id(1) == 0)
    def _():
        ssq_ref[...] = part

    @pl.when(pl.program_id(1) > 0)
    def _():
        ssq_ref[...] += part


def _outproj_residual(a, w_bf16, res, gain, tm=1024, tn=512):
    s, kdim = a.shape
    n = w_bf16.shape[1]
    block_bytes = tm * kdim * 2 + kdim * tn * 2 + tm * tn * (4 + 4 + 2) + tn * 4 + tm * LANES * 4
    tile = pl.BlockSpec((tm, tn), lambda m, j: (m, j))
    return pl.pallas_call(
        _outproj_kernel,
        grid=(s // tm, n // tn),
        in_specs=[pl.BlockSpec((tm, kdim), lambda m, j: (m, 0)),
                  pl.BlockSpec((kdim, tn), lambda m, j: (0, j)),
                  tile,
                  pl.BlockSpec((1, tn), lambda m, j: (0, j))],
        out_specs=[tile, tile, pl.BlockSpec((tm, LANES), lambda m, j: (m, 0))],
        out_shape=[jax.ShapeDtypeStruct((s, n), F32), jax.ShapeDtypeStruct((s, n), BF16),
                   jax.ShapeDtypeStruct((s, LANES), F32)],
        compiler_params=pltpu.CompilerParams(
            dimension_semantics=("arbitrary", "arbitrary"),
            vmem_limit_bytes=_vmem_limit(block_bytes, 0, 3 * tm * tn * 4)),
        name="attn_outproj",
    )(a, w_bf16, res, gain)


def _outproj_norm_kernel(a_ref, w_ref, res_ref, gain_ref, o_ref):
    kk = pl.program_id(1)

    @pl.when(kk == 0)
    def _():
        o_ref[...] = res_ref[...]

    o_ref[...] += jnp.dot(a_ref[...], w_ref[...], preferred_element_type=F32)

    @pl.when(kk == pl.num_programs(1) - 1)
    def _():
        _rmsnorm_rows(o_ref, gain_ref, o_ref, o_ref.shape[0])


def _outproj_residual_norm(a, w_bf16, res, gain, tm=512, tk=1024):
    s, kdim = a.shape
    n = w_bf16.shape[1]
    block_bytes = tm * tk * 2 + tk * n * 2 + 2 * tm * n * 4 + n * 4
    return pl.pallas_call(
        _outproj_norm_kernel,
        grid=(s // tm, kdim // tk),
        in_specs=[pl.BlockSpec((tm, tk), lambda m, kk: (m, kk)),
                  pl.BlockSpec((tk, n), lambda m, kk: (kk, 0)),
                  pl.BlockSpec((tm, n), lambda m, kk: (m, 0)),
                  pl.BlockSpec((1, n), lambda m, kk: (0, 0))],
        out_specs=pl.BlockSpec((tm, n), lambda m, kk: (m, 0)),
        out_shape=jax.ShapeDtypeStruct((s, n), F32),
        compiler_params=pltpu.CompilerParams(
            dimension_semantics=("arbitrary", "arbitrary"),
            vmem_limit_bytes=_vmem_limit(block_bytes, 0, tm * n * 4)),
        name="conv_outproj_norm",
    )(a, w_bf16, res, gain)


def kernel(x, norm_attn, w_in_attn, w_out_attn, norm_conv, w_in_conv, conv_w, w_out_conv, final_norm):
    b, s, d = x.shape
    assert b == 1 and d == N_HEADS * HEAD_DIM
    x2 = x.reshape(s, d)
    row = lambda g: g.reshape(1, d).astype(F32)

    q, k, v, sg, wo_attn_bf, wo_conv_bf = _attn_inproj(
        _rmsnorm_bf16(x2, row(norm_attn)), w_in_attn, w_out_attn, w_out_conv)
    og = _attention(q, k, v, sg)
    h1, h1g, ssq = _outproj_residual(og, wo_attn_bf, x2, row(norm_conv))
    yg = _conv_inproj(h1g, ssq, w_in_conv, conv_w.astype(F32))
    out = _outproj_residual_norm(yg, wo_conv_bf, h1, row(final_norm))
    return out.reshape(b, s, d)
```

```python
import functools
import math

import jax
import jax.numpy as jnp
from jax import lax
from jax.experimental import pallas as pl
from jax.experimental.pallas import tpu as pltpu

F32 = jnp.float32
BF16 = jnp.bfloat16

RMS_EPS = 1e-6
N_HEADS = 32
HEAD_DIM = 128
CONV_K = 3

LANES = 128
SUBLANES = 8
VMEM_BYTES_V7X = 64 * 1024 * 1024

QB = 128
KB = 128
WIN = 3
HEADS_PER_STEP = 4
Q_ROWS_PER_STEP = 2048
Q_BLOCKS_PER_ITER = 2
LOG2_E = math.log2(math.e)
SCORE_SCALE2 = LOG2_E / math.sqrt(HEAD_DIM)
LOG2_ZERO_F32 = -152.0

NORM_ROWS = 16
NORM_UNROLL = 4


def _vmem_limit(block_bytes, scratch_bytes, temp_bytes):
    need = 2 * block_bytes + scratch_bytes + temp_bytes + (4 << 20)
    return int(min(need, VMEM_BYTES_V7X - (4 << 20)))


def _silu(g):
    return g * (1.0 / (1.0 + jnp.exp(-g)))


def _rmsnorm_rows(x_ref, gain_ref, dst_ref, rows):
    group = NORM_ROWS * NORM_UNROLL

    def body(r, _):
        base = r * group
        sls = [pl.ds(pl.multiple_of(base + j * NORM_ROWS, NORM_ROWS), NORM_ROWS) for j in range(NORM_UNROLL)]
        inv = []
        for sl in sls:
            xf = x_ref[sl, :]
            inv.append(lax.rsqrt(jnp.mean(xf * xf, axis=-1, keepdims=True) + RMS_EPS))
        for sl, rs in zip(sls, inv):
            dst_ref[sl, :] = (x_ref[sl, :] * rs * gain_ref[...]).astype(dst_ref.dtype)
        return 0
    lax.fori_loop(0, rows // group, body, 0)


def _norm_kernel(x_ref, gain_ref, o_ref):
    _rmsnorm_rows(x_ref, gain_ref, o_ref, x_ref.shape[0])


def _rmsnorm_bf16(x, gain, tm=512):
    s, d = x.shape
    return pl.pallas_call(
        _norm_kernel,
        grid=(s // tm,),
        in_specs=[pl.BlockSpec((tm, d), lambda m: (m, 0)), pl.BlockSpec((1, d), lambda m: (0, 0))],
        out_specs=pl.BlockSpec((tm, d), lambda m: (m, 0)),
        out_shape=jax.ShapeDtypeStruct((s, d), BF16),
        compiler_params=pltpu.CompilerParams(
            dimension_semantics=("arbitrary",),
            vmem_limit_bytes=_vmem_limit(tm * d * (4 + 2) + d * 4, 0, 4 << 20)),
        name="rmsnorm_bf16",
    )(x, gain)


N_PROJ = 4
CAST_ROWS = 256


def _stage_weights(w_hbm, stage_ref, wbf_ref, sem, c, nblk):
    tn = stage_ref.shape[2]
    width = nblk * tn

    def copy(j, cc):
        col = pl.multiple_of(j * width + cc * tn, tn)
        return pltpu.make_async_copy(w_hbm.at[:, pl.ds(col, tn)], stage_ref.at[j], sem.at[j])

    @pl.when(c == 0)
    def _():
        for j in range(N_PROJ):
            copy(j, c).start()

    for j in range(N_PROJ):
        copy(j, c).wait()

    def cast(r, _):
        rows = pl.ds(pl.multiple_of(r * CAST_ROWS, CAST_ROWS), CAST_ROWS)
        for j in range(N_PROJ):
            wbf_ref[j, rows, :] = stage_ref[j, rows, :].astype(BF16)
        return 0
    lax.fori_loop(0, stage_ref.shape[1] // CAST_ROWS, cast, 0)

    @pl.when(c + 1 < nblk)
    def _():
        for j in range(N_PROJ):
            copy(j, c + 1).start()


def _proj(xn_ref, wbf_ref, j):
    return jnp.dot(xn_ref[...], wbf_ref[j], preferred_element_type=F32)


def _row_rsqrt(ssq_ref, d):
    return lax.rsqrt(jnp.sum(ssq_ref[...], axis=1, keepdims=True) * (1.0 / d) + RMS_EPS)


def _attn_inproj_kernel(xn_ref, w_hbm, wo_attn_ref, wo_conv_ref, q_ref, k_ref, v_ref, sg_ref,
                        wo_attn_bf_ref, wo_conv_bf_ref, stage_ref, wbf_ref, sem):
    @pl.when(pl.program_id(1) == 0)
    def _():
        _stage_weights(w_hbm, stage_ref, wbf_ref, sem, pl.program_id(0), pl.num_programs(0))

    wo_attn_bf_ref[...] = wo_attn_ref[...].astype(BF16)
    wo_conv_bf_ref[...] = wo_conv_ref[...].astype(BF16)

    heads = [slice(hh * HEAD_DIM, (hh + 1) * HEAD_DIM) for hh in range(q_ref.shape[0])]
    sg = _silu(_proj(xn_ref, wbf_ref, 3))
    for hh, cols in enumerate(heads):
        sg_ref[hh] = sg[:, cols]
    for j, (o_ref, mult) in enumerate(((q_ref, SCORE_SCALE2), (k_ref, None), (v_ref, None))):
        p = _proj(xn_ref, wbf_ref, j)
        if mult is not None:
            p = p * mult
        for hh, cols in enumerate(heads):
            o_ref[hh] = p[:, cols].astype(BF16)


def _conv_inproj_kernel(hg_ref, ssq_ref, w_hbm, cw_ref, y_ref, stage_ref, wbf_ref, sem, halo_ref):
    tm, d = hg_ref.shape
    m = pl.program_id(1)
    xn_ref = hg_ref

    @pl.when(m == 0)
    def _():
        _stage_weights(w_hbm, stage_ref, wbf_ref, sem, pl.program_id(0), pl.num_programs(0))

    rs = _row_rsqrt(ssq_ref, d)
    sg = _silu(rs * _proj(xn_ref, wbf_ref, 3))
    cu = (rs * _proj(xn_ref, wbf_ref, 1)) * (rs * _proj(xn_ref, wbf_ref, 2))

    halo = jnp.where(m > 0, halo_ref[...], 0.0)
    prev1 = halo[SUBLANES - 1:SUBLANES, :]
    prev2 = halo[SUBLANES - 2:SUBLANES - 1, :]
    row = lax.broadcasted_iota(jnp.int32, cu.shape, 0)
    cu1 = jnp.where(row == 0, prev1, pltpu.roll(cu, 1, 0))
    cu2 = jnp.where(row == 0, prev2, jnp.where(row == 1, prev1, pltpu.roll(cu, 2, 0)))
    halo_ref[...] = cu[tm - SUBLANES:, :]

    cw = cw_ref[...]
    conv = cw[0:1, :] * cu2 + cw[1:2, :] * cu1 + cw[2:3, :] * cu
    gb = rs * _proj(xn_ref, wbf_ref, 0)
    y_ref[...] = ((gb * conv) * sg).astype(y_ref.dtype)


def _inproj_call(kernel_fn, name, xn, w, extra_in, pre_in, pre_specs, extra_specs, out_specs, out_shape,
                 out_block_bytes, extra_scratch, tm, tn):
    s, d = xn.shape
    nblk = w.shape[1] // N_PROJ // tn
    x_spec = pl.BlockSpec((tm, d), lambda c, m: (m, 0))
    w_spec = pl.BlockSpec(memory_space=pl.ANY)
    scratch = [pltpu.VMEM((N_PROJ, d, tn), F32), pltpu.VMEM((N_PROJ, d, tn), BF16),
               pltpu.SemaphoreType.DMA((N_PROJ,))] + extra_scratch
    scratch_bytes = N_PROJ * d * tn * (4 + 2)
    temp_bytes = 8 * tm * tn * 4
    return pl.pallas_call(
        kernel_fn,
        grid=(nblk, s // tm),
        in_specs=[x_spec] + pre_specs + [w_spec] + extra_specs,
        out_specs=out_specs,
        out_shape=out_shape,
        scratch_shapes=scratch,
        compiler_params=pltpu.CompilerParams(
            dimension_semantics=("arbitrary", "arbitrary"),
            vmem_limit_bytes=_vmem_limit(tm * d * 2 + out_block_bytes, scratch_bytes, temp_bytes)),
        name=name,
    )(xn, *pre_in, w, *extra_in)


def _attn_inproj(xn, w, w_out_attn, w_out_conv, tm=1024, tn=256):
    s = xn.shape[0]
    hpt = tn // HEAD_DIM
    head_spec = pl.BlockSpec((hpt, tm, HEAD_DIM), lambda c, m: (c, m, 0))
    nm = s // tm
    kdim, n = w_out_attn.shape
    assert w_out_conv.shape == (kdim, n)
    slab = kdim // ((w.shape[1] // N_PROJ // tn) * nm)
    slab_spec = pl.BlockSpec((slab, n), lambda c, m: (c * nm + m, 0))
    return _inproj_call(
        _attn_inproj_kernel, "attn_inproj", xn, w, [w_out_attn, w_out_conv], [], [], [slab_spec] * 2,
        [head_spec] * 4 + [slab_spec] * 2,
        [jax.ShapeDtypeStruct((N_HEADS, s, HEAD_DIM), BF16)] * 3
        + [jax.ShapeDtypeStruct((N_HEADS, s, HEAD_DIM), F32)]
        + [jax.ShapeDtypeStruct((kdim, n), BF16)] * 2,
        hpt * tm * HEAD_DIM * (3 * 2 + 4) + 2 * slab * n * (4 + 2), [], tm, tn)


def _conv_inproj(hg, ssq, w, conv_w, tm=1024, tn=256):
    s = hg.shape[0]
    width = w.shape[1] // N_PROJ
    return _inproj_call(
        _conv_inproj_kernel, "conv_inproj", hg, w, [conv_w], [ssq],
        [pl.BlockSpec((tm, LANES), lambda c, m: (m, 0))],
        [pl.BlockSpec((CONV_K, tn), lambda c, m: (0, c))],
        pl.BlockSpec((tm, tn), lambda c, m: (m, c)),
        jax.ShapeDtypeStruct((s, width), BF16),
        tm * tn * 2, [pltpu.VMEM((SUBLANES, tn), F32)], tm, tn)


def _softplus2(z2):
    return jnp.maximum(z2, 0.0) + jnp.log(1.0 + jnp.exp2(-jnp.abs(z2))) * LOG2_E


def _split_hi_lo(x):
    hi = x.astype(BF16)
    lo = (x - hi.astype(F32)).astype(BF16)
    return jnp.concatenate([hi, lo], axis=1)


def _attn_kernel(q_ref, k_ref, v_ref, sg_ref, o_ref, mm_ref):
    n_heads, sq, dh = q_ref.shape
    nqb = sq // QB
    first_block = pl.program_id(1) * nqb
    nt = (((1,), (1,)), ((), ()))

    r = lax.broadcasted_iota(jnp.int32, (2 * KB, 2 * KB), 0) & (KB - 1)
    cc = lax.broadcasted_iota(jnp.int32, (2 * KB, 2 * KB), 1)
    mm_ref[...] = jnp.where((cc >= KB) | (r >= cc), -1.0, 0.0).astype(BF16)

    def scores(q, kblk):
        return lax.dot_general(q, kblk, nt, preferred_element_type=F32)

    def causal_mask():
        row = lax.broadcasted_iota(jnp.int32, (QB, KB), 0)
        col = lax.broadcasted_iota(jnp.int32, (QB, KB), 1)
        return col < row

    def windows(chains, tri, nblk):
        wss = [pl.multiple_of((ib - (nblk - 1)) * KB, KB) for _, _, ib in chains]
        zs = [scores(q_ref[g, pl.ds(r0, QB), :], k_ref[g, pl.ds(ws, nblk * KB), :])
              for (g, r0, _), ws in zip(chains, wss)]
        lhs = []
        for z in zs:
            sp = _softplus2(z)
            sps = [sp[:, b * KB:(b + 1) * KB] for b in range(nblk)]
            sps[-1] = jnp.where(tri, sps[-1], 0.0)
            lhs.append(jnp.concatenate([_split_hi_lo(s) for s in sps], axis=0))
        ps = [jnp.dot(l, mm_ref[...], preferred_element_type=F32) for l in lhs]
        a_all, carries = [], []
        for z, p in zip(zs, ps):
            carry = None
            a_blocks = [None] * nblk
            for b in reversed(range(nblk)):
                pb = p[b * QB:(b + 1) * QB]
                log2_a = z[:, b * KB:(b + 1) * KB] + pb[:, :KB]
                if carry is not None:
                    log2_a = log2_a + carry
                a = jnp.exp2(log2_a)
                if b == nblk - 1:
                    a = jnp.where(tri, a, 0.0)
                a_blocks[b] = a.astype(BF16)
                carry = pb[:, KB:] if carry is None else carry + pb[:, KB:]
            a_all.append(jnp.concatenate(a_blocks, axis=1))
            carries.append(carry)
        accs = [jnp.dot(a, v_ref[g, pl.ds(ws, nblk * KB), :], preferred_element_type=F32)
                for a, (g, _, _), ws in zip(a_all, chains, wss)]
        return list(zip(carries, accs))

    def sweep_rest(g, r0, kb, carry, acc):
        q = q_ref[g, pl.ds(r0, QB), :]

        def cond(st):
            kb, _, _, mx = st
            return jnp.logical_and(kb >= 0, mx > LOG2_ZERO_F32)

        def body(st):
            kb, carry, acc, _ = st
            k0 = pl.multiple_of(kb * KB, KB)
            z = scores(q, k_ref[g, pl.ds(k0, KB), :])
            p = jnp.dot(_split_hi_lo(_softplus2(z)), mm_ref[...], preferred_element_type=F32)
            a = jnp.exp2(z + p[:, :KB] + carry)
            acc = acc + jnp.dot(a.astype(BF16), v_ref[g, pl.ds(k0, KB), :], preferred_element_type=F32)
            carry = carry + p[:, KB:]
            return kb - 1, carry, acc, jnp.max(carry)

        _, _, acc, _ = lax.while_loop(cond, body, (kb, carry, acc, jnp.max(carry)))
        return acc

    def qblocks(ils, nblk):
        chains = []
        for il in ils:
            r0 = il * QB if isinstance(il, int) else pl.multiple_of(il * QB, QB)
            chains += [(g, r0, first_block + il) for g in range(n_heads)]
        res = windows(chains, causal_mask(), nblk)
        live = jnp.max(functools.reduce(jnp.maximum, [c for c, _ in res])) > LOG2_ZERO_F32
        accs = lax.cond(
            live,
            lambda: [sweep_rest(g, r0, ib - nblk, *rs) for (g, r0, ib), rs in zip(chains, res)],
            lambda: [a for _, a in res])
        for (g, r0, _), acc in zip(chains, accs):
            gated = acc * sg_ref[g, pl.ds(r0, QB), :]
            o_ref[pl.ds(r0, QB), g * dh:(g + 1) * dh] = gated.astype(o_ref.dtype)

    @pl.when(first_block == 0)
    def _():
        for il in range(WIN - 1):
            qblocks([il], il + 1)

    def body(it, _):
        qblocks([it * Q_BLOCKS_PER_ITER + j for j in range(Q_BLOCKS_PER_ITER)], WIN)
        return 0

    assert (WIN - 1) % Q_BLOCKS_PER_ITER == 0 and nqb % Q_BLOCKS_PER_ITER == 0
    lax.fori_loop(jnp.where(first_block == 0, (WIN - 1) // Q_BLOCKS_PER_ITER, 0),
                  nqb // Q_BLOCKS_PER_ITER, body, 0)


def _attention(q, k, v, sg, heads=HEADS_PER_STEP, sq=Q_ROWS_PER_STEP):
    h, s, dh = q.shape
    q_spec = pl.BlockSpec((heads, sq, dh), lambda hg, j: (hg, j, 0))
    kv_spec = pl.BlockSpec((heads, s, dh), lambda hg, j: (hg, 0, 0))
    block_bytes = heads * dh * (sq * (2 + 4 + 2) + 2 * s * 2)
    return pl.pallas_call(
        _attn_kernel,
        grid=(h // heads, s // sq),
        in_specs=[q_spec, kv_spec, kv_spec, q_spec],
        out_specs=pl.BlockSpec((sq, heads * dh), lambda hg, j: (j, hg)),
        out_shape=jax.ShapeDtypeStruct((s, h * dh), BF16),
        scratch_shapes=[pltpu.VMEM((2 * KB, 2 * KB), BF16)],
        compiler_params=pltpu.CompilerParams(
            dimension_semantics=("arbitrary", "arbitrary"),
            vmem_limit_bytes=_vmem_limit(block_bytes, 4 * KB * KB * 2, 8 << 20)),
        name="stickbreak_attn",
    )(q, k, v, sg)


def _outproj_kernel(a_ref, w_ref, res_ref, gain_ref, h_ref, hg_ref, ssq_ref):
    half = w_ref.shape[1] // 2
    part = None
    for j in range(2):
        cols = slice(j * half, (j + 1) * half)
        h = res_ref[:, cols] + jnp.dot(a_ref[...], w_ref[:, cols], preferred_element_type=F32)
        h_ref[:, cols] = h
        hg_ref[:, cols] = (h * gain_ref[:, cols]).astype(hg_ref.dtype)
        hh = h * h
        for g in range(half // LANES):
            lanes = hh[:, g * LANES:(g + 1) * LANES]
            part = lanes if part is None else part + lanes

    @pl.when(pl.program_id(1) == 0)
    def _():
        ssq_ref[...] = part

    @pl.when(pl.program_id(1) > 0)
    def _():
        ssq_ref[...] += part


def _outproj_residual(a, w_bf16, res, gain, tm=1024, tn=512):
    s, kdim = a.shape
    n = w_bf16.shape[1]
    block_bytes = tm * kdim * 2 + kdim * tn * 2 + tm * tn * (4 + 4 + 2) + tn * 4 + tm * LANES * 4
    tile = pl.BlockSpec((tm, tn), lambda m, j: (m, j))
    return pl.pallas_call(
        _outproj_kernel,
        grid=(s // tm, n // tn),
        in_specs=[pl.BlockSpec((tm, kdim), lambda m, j: (m, 0)),
                  pl.BlockSpec((kdim, tn), lambda m, j: (0, j)),
                  tile,
                  pl.BlockSpec((1, tn), lambda m, j: (0, j))],
        out_specs=[tile, tile, pl.BlockSpec((tm, LANES), lambda m, j: (m, 0))],
        out_shape=[jax.ShapeDtypeStruct((s, n), F32), jax.ShapeDtypeStruct((s, n), BF16),
                   jax.ShapeDtypeStruct((s, LANES), F32)],
        compiler_params=pltpu.CompilerParams(
            dimension_semantics=("arbitrary", "arbitrary"),
            vmem_limit_bytes=_vmem_limit(block_bytes, 0, 3 * tm * tn * 4)),
        name="attn_outproj",
    )(a, w_bf16, res, gain)


def _outproj_norm_kernel(a_ref, w_ref, res_ref, gain_ref, o_ref):
    n = w_ref.shape[1]
    half = n // 2
    ssq = None
    for j in range(2):
        cols = slice(j * half, (j + 1) * half)
        h = res_ref[:, cols] + jnp.dot(a_ref[...], w_ref[:, cols], preferred_element_type=F32)
        o_ref[:, cols] = h
        part = jnp.sum(h * h, axis=1, keepdims=True)
        ssq = part if ssq is None else ssq + part
    rs = lax.rsqrt(ssq * (1.0 / n) + RMS_EPS)
    o_ref[...] = o_ref[...] * rs * gain_ref[...]


def _outproj_residual_norm(a, w_bf16, res, gain, tm=256):
    s, kdim = a.shape
    n = w_bf16.shape[1]
    rows = lambda width: pl.BlockSpec((tm, width), lambda m: (m, 0))
    w_spec = pl.BlockSpec((kdim, n), lambda m: (0, 0), pipeline_mode=pl.Buffered(1))
    need = kdim * n * 2 + 2 * (tm * kdim * 2 + 2 * tm * n * 4 + n * 4) + 2 * tm * n * 4 + (2 << 20)
    return pl.pallas_call(
        _outproj_norm_kernel,
        grid=(s // tm,),
        in_specs=[rows(kdim), w_spec, rows(n), pl.BlockSpec((1, n), lambda m: (0, 0))],
        out_specs=rows(n),
        out_shape=jax.ShapeDtypeStruct((s, n), F32),
        compiler_params=pltpu.CompilerParams(
            dimension_semantics=("arbitrary",),
            vmem_limit_bytes=int(min(need, VMEM_BYTES_V7X - (4 << 20)))),
        name="conv_outproj_norm",
    )(a, w_bf16, res, gain)


def kernel(x, norm_attn, w_in_attn, w_out_attn, norm_conv, w_in_conv, conv_w, w_out_conv, final_norm):
    b, s, d = x.shape
    assert b == 1 and d == N_HEADS * HEAD_DIM
    x2 = x.reshape(s, d)
    row = lambda g: g.reshape(1, d).astype(F32)

    q, k, v, sg, wo_attn_bf, wo_conv_bf = _attn_inproj(
        _rmsnorm_bf16(x2, row(norm_attn)), w_in_attn, w_out_attn, w_out_conv)
    og = _attention(q, k, v, sg)
    h1, h1g, ssq = _outproj_residual(og, wo_attn_bf, x2, row(norm_conv))
    yg = _conv_inproj(h1g, ssq, w_in_conv, conv_w.astype(F32))
    out = _outproj_residual_norm(yg, wo_conv_bf, h1, row(final_norm))
    return out.reshape(b, s, d)
```

```python
import functools
import math

import jax
import jax.numpy as jnp
from jax import lax
from jax.experimental import pallas as pl
from jax.experimental.pallas import tpu as pltpu

F32 = jnp.float32
BF16 = jnp.bfloat16

RMS_EPS = 1e-6
N_HEADS = 32
HEAD_DIM = 128
CONV_K = 3

LANES = 128
SUBLANES = 8
VMEM_BYTES_V7X = 64 * 1024 * 1024

QB = 128
KB = 128
WIN = 3
HEADS_PER_STEP = 4
Q_ROWS_PER_STEP = 2048
Q_BLOCKS_PER_ITER = 2
FAR_ROWS = QB // 2
LOG2_E = math.log2(math.e)
SCORE_SCALE2 = LOG2_E / math.sqrt(HEAD_DIM)
LOG2_ZERO_F32 = -152.0

NORM_ROWS = 16
NORM_UNROLL = 4


def _vmem_limit(block_bytes, scratch_bytes, temp_bytes):
    need = 2 * block_bytes + scratch_bytes + temp_bytes + (4 << 20)
    return int(min(need, VMEM_BYTES_V7X - (4 << 20)))


def _silu(g):
    return g * (1.0 / (1.0 + jnp.exp(-g)))


def _rmsnorm_rows(x_ref, gain_ref, dst_ref, rows):
    group = NORM_ROWS * NORM_UNROLL

    def body(r, _):
        base = r * group
        sls = [pl.ds(pl.multiple_of(base + j * NORM_ROWS, NORM_ROWS), NORM_ROWS) for j in range(NORM_UNROLL)]
        inv = []
        for sl in sls:
            xf = x_ref[sl, :]
            inv.append(lax.rsqrt(jnp.mean(xf * xf, axis=-1, keepdims=True) + RMS_EPS))
        for sl, rs in zip(sls, inv):
            dst_ref[sl, :] = (x_ref[sl, :] * rs * gain_ref[...]).astype(dst_ref.dtype)
        return 0
    lax.fori_loop(0, rows // group, body, 0)


def _norm_kernel(x_ref, gain_ref, o_ref):
    _rmsnorm_rows(x_ref, gain_ref, o_ref, x_ref.shape[0])


def _rmsnorm_bf16(x, gain, tm=512):
    s, d = x.shape
    return pl.pallas_call(
        _norm_kernel,
        grid=(s // tm,),
        in_specs=[pl.BlockSpec((tm, d), lambda m: (m, 0)), pl.BlockSpec((1, d), lambda m: (0, 0))],
        out_specs=pl.BlockSpec((tm, d), lambda m: (m, 0)),
        out_shape=jax.ShapeDtypeStruct((s, d), BF16),
        compiler_params=pltpu.CompilerParams(
            dimension_semantics=("arbitrary",),
            vmem_limit_bytes=_vmem_limit(tm * d * (4 + 2) + d * 4, 0, 4 << 20)),
        name="rmsnorm_bf16",
    )(x, gain)


N_PROJ = 4
CAST_ROWS = 256


def _stage_weights(w_hbm, stage_ref, wbf_ref, sem, c, nblk):
    tn = stage_ref.shape[2]
    width = nblk * tn

    def copy(j, cc):
        col = pl.multiple_of(j * width + cc * tn, tn)
        return pltpu.make_async_copy(w_hbm.at[:, pl.ds(col, tn)], stage_ref.at[j], sem.at[j])

    @pl.when(c == 0)
    def _():
        for j in range(N_PROJ):
            copy(j, c).start()

    for j in range(N_PROJ):
        copy(j, c).wait()

    def cast(r, _):
        rows = pl.ds(pl.multiple_of(r * CAST_ROWS, CAST_ROWS), CAST_ROWS)
        for j in range(N_PROJ):
            wbf_ref[j, rows, :] = stage_ref[j, rows, :].astype(BF16)
        return 0
    lax.fori_loop(0, stage_ref.shape[1] // CAST_ROWS, cast, 0)

    @pl.when(c + 1 < nblk)
    def _():
        for j in range(N_PROJ):
            copy(j, c + 1).start()


def _proj(xn_ref, wbf_ref, j):
    return jnp.dot(xn_ref[...], wbf_ref[j], preferred_element_type=F32)


def _row_rsqrt(ssq_ref, d):
    return lax.rsqrt(jnp.sum(ssq_ref[...], axis=1, keepdims=True) * (1.0 / d) + RMS_EPS)


def _attn_inproj_kernel(xn_ref, w_hbm, wo_attn_ref, wo_conv_ref, q_ref, k_ref, v_ref, sg_ref,
                        wo_attn_bf_ref, wo_conv_bf_ref, stage_ref, wbf_ref, sem):
    @pl.when(pl.program_id(1) == 0)
    def _():
        _stage_weights(w_hbm, stage_ref, wbf_ref, sem, pl.program_id(0), pl.num_programs(0))

    wo_attn_bf_ref[...] = wo_attn_ref[...].astype(BF16)
    wo_conv_bf_ref[...] = wo_conv_ref[...].astype(BF16)

    heads = [slice(hh * HEAD_DIM, (hh + 1) * HEAD_DIM) for hh in range(q_ref.shape[0])]
    sg = _silu(_proj(xn_ref, wbf_ref, 3))
    for hh, cols in enumerate(heads):
        sg_ref[hh] = sg[:, cols]
    for j, (o_ref, mult) in enumerate(((q_ref, SCORE_SCALE2), (k_ref, None), (v_ref, None))):
        p = _proj(xn_ref, wbf_ref, j)
        if mult is not None:
            p = p * mult
        for hh, cols in enumerate(heads):
            o_ref[hh] = p[:, cols].astype(BF16)


def _conv_inproj_kernel(hg_ref, ssq_ref, w_hbm, cw_ref, y_ref, stage_ref, wbf_ref, sem, halo_ref):
    tm, d = hg_ref.shape
    m = pl.program_id(1)
    xn_ref = hg_ref

    @pl.when(m == 0)
    def _():
        _stage_weights(w_hbm, stage_ref, wbf_ref, sem, pl.program_id(0), pl.num_programs(0))

    rs = _row_rsqrt(ssq_ref, d)
    sg = _silu(rs * _proj(xn_ref, wbf_ref, 3))
    cu = (rs * _proj(xn_ref, wbf_ref, 1)) * (rs * _proj(xn_ref, wbf_ref, 2))

    halo = jnp.where(m > 0, halo_ref[...], 0.0)
    prev1 = halo[SUBLANES - 1:SUBLANES, :]
    prev2 = halo[SUBLANES - 2:SUBLANES - 1, :]
    row = lax.broadcasted_iota(jnp.int32, cu.shape, 0)
    cu1 = jnp.where(row == 0, prev1, pltpu.roll(cu, 1, 0))
    cu2 = jnp.where(row == 0, prev2, jnp.where(row == 1, prev1, pltpu.roll(cu, 2, 0)))
    halo_ref[...] = cu[tm - SUBLANES:, :]

    cw = cw_ref[...]
    conv = cw[0:1, :] * cu2 + cw[1:2, :] * cu1 + cw[2:3, :] * cu
    gb = rs * _proj(xn_ref, wbf_ref, 0)
    y_ref[...] = ((gb * conv) * sg).astype(y_ref.dtype)


def _inproj_call(kernel_fn, name, xn, w, extra_in, pre_in, pre_specs, extra_specs, out_specs, out_shape,
                 out_block_bytes, extra_scratch, tm, tn):
    s, d = xn.shape
    nblk = w.shape[1] // N_PROJ // tn
    x_spec = pl.BlockSpec((tm, d), lambda c, m: (m, 0))
    w_spec = pl.BlockSpec(memory_space=pl.ANY)
    scratch = [pltpu.VMEM((N_PROJ, d, tn), F32), pltpu.VMEM((N_PROJ, d, tn), BF16),
               pltpu.SemaphoreType.DMA((N_PROJ,))] + extra_scratch
    scratch_bytes = N_PROJ * d * tn * (4 + 2)
    temp_bytes = 8 * tm * tn * 4
    return pl.pallas_call(
        kernel_fn,
        grid=(nblk, s // tm),
        in_specs=[x_spec] + pre_specs + [w_spec] + extra_specs,
        out_specs=out_specs,
        out_shape=out_shape,
        scratch_shapes=scratch,
        compiler_params=pltpu.CompilerParams(
            dimension_semantics=("arbitrary", "arbitrary"),
            vmem_limit_bytes=_vmem_limit(tm * d * 2 + out_block_bytes, scratch_bytes, temp_bytes)),
        name=name,
    )(xn, *pre_in, w, *extra_in)


def _attn_inproj(xn, w, w_out_attn, w_out_conv, tm=1024, tn=256):
    s = xn.shape[0]
    hpt = tn // HEAD_DIM
    head_spec = pl.BlockSpec((hpt, tm, HEAD_DIM), lambda c, m: (c, m, 0))
    nm = s // tm
    kdim, n = w_out_attn.shape
    assert w_out_conv.shape == (kdim, n)
    slab = kdim // ((w.shape[1] // N_PROJ // tn) * nm)
    slab_spec = pl.BlockSpec((slab, n), lambda c, m: (c * nm + m, 0))
    return _inproj_call(
        _attn_inproj_kernel, "attn_inproj", xn, w, [w_out_attn, w_out_conv], [], [], [slab_spec] * 2,
        [head_spec] * 4 + [slab_spec] * 2,
        [jax.ShapeDtypeStruct((N_HEADS, s, HEAD_DIM), BF16)] * 3
        + [jax.ShapeDtypeStruct((N_HEADS, s, HEAD_DIM), F32)]
        + [jax.ShapeDtypeStruct((kdim, n), BF16)] * 2,
        hpt * tm * HEAD_DIM * (3 * 2 + 4) + 2 * slab * n * (4 + 2), [], tm, tn)


def _conv_inproj(hg, ssq, w, conv_w, tm=1024, tn=256):
    s = hg.shape[0]
    width = w.shape[1] // N_PROJ
    return _inproj_call(
        _conv_inproj_kernel, "conv_inproj", hg, w, [conv_w], [ssq],
        [pl.BlockSpec((tm, LANES), lambda c, m: (m, 0))],
        [pl.BlockSpec((CONV_K, tn), lambda c, m: (0, c))],
        pl.BlockSpec((tm, tn), lambda c, m: (m, c)),
        jax.ShapeDtypeStruct((s, width), BF16),
        tm * tn * 2, [pltpu.VMEM((SUBLANES, tn), F32)], tm, tn)


def _softplus2(z2):
    return jnp.maximum(z2, 0.0) + jnp.log(1.0 + jnp.exp2(-jnp.abs(z2))) * LOG2_E


def _split_hi_lo(x):
    hi = x.astype(BF16)
    lo = (x - hi.astype(F32)).astype(BF16)
    return jnp.concatenate([hi, lo], axis=1)


def _attn_kernel(q_ref, k_ref, v_ref, sg_ref, o_ref, mm_ref):
    n_heads, sq, dh = q_ref.shape
    nqb = sq // QB
    first_block = pl.program_id(1) * nqb
    nt = (((1,), (1,)), ((), ()))

    r = lax.broadcasted_iota(jnp.int32, (2 * KB, 2 * KB), 0) & (KB - 1)
    cc = lax.broadcasted_iota(jnp.int32, (2 * KB, 2 * KB), 1)
    mm_ref[...] = jnp.where((cc >= KB) | (r >= cc), -1.0, 0.0).astype(BF16)

    def scores(q, kblk):
        return lax.dot_general(q, kblk, nt, preferred_element_type=F32)

    def causal_mask():
        row = lax.broadcasted_iota(jnp.int32, (QB, KB), 0)
        col = lax.broadcasted_iota(jnp.int32, (QB, KB), 1)
        return col < row

    def windows(chains, tri, nblk, far_rows=QB):
        split_far = far_rows < QB
        near = nblk - 1 if split_far else nblk
        far0 = far_rows if split_far else 0
        wss = [pl.multiple_of((ib - (nblk - 1)) * KB, KB) for _, _, ib in chains]
        nss = [pl.multiple_of((ib - (near - 1)) * KB, KB) for _, _, ib in chains]
        zs = [scores(q_ref[g, pl.ds(r0, QB), :], k_ref[g, pl.ds(ns, near * KB), :])
              for (g, r0, _), ns in zip(chains, nss)]
        zfs = [scores(q_ref[g, pl.ds(r0, far_rows), :], k_ref[g, pl.ds(ws, KB), :])
               for (g, r0, _), ws in zip(chains, wss)] if split_far else [None] * len(chains)
        lhs = []
        for z, zf in zip(zs, zfs):
            sp = _softplus2(z)
            sps = [sp[:, b * KB:(b + 1) * KB] for b in range(near)]
            sps[-1] = jnp.where(tri, sps[-1], 0.0)
            if split_far:
                sps = [_softplus2(zf)] + sps
            lhs.append(jnp.concatenate([_split_hi_lo(s) for s in sps], axis=0))
        ps = [jnp.dot(l, mm_ref[...], preferred_element_type=F32) for l in lhs]
        a_all, a_far, carries = [], [], []
        for z, zf, p in zip(zs, zfs, ps):
            carry = None
            a_blocks = [None] * near
            for b in reversed(range(near)):
                pb = p[far0 + b * QB:far0 + (b + 1) * QB]
                log2_a = z[:, b * KB:(b + 1) * KB] + pb[:, :KB]
                if carry is not None:
                    log2_a = log2_a + carry
                a = jnp.exp2(log2_a)
                if b == near - 1:
                    a = jnp.where(tri, a, 0.0)
                a_blocks[b] = a.astype(BF16)
                carry = pb[:, KB:] if carry is None else carry + pb[:, KB:]
            if split_far:
                pf = p[:far0]
                a_far.append(jnp.exp2(zf + pf[:, :KB] + carry[:far0]).astype(BF16))
                carry = jnp.concatenate([carry[:far0] + pf[:, KB:], carry[far0:]], axis=0)
            a_all.append(jnp.concatenate(a_blocks, axis=1))
            carries.append(carry)
        accs = [jnp.dot(a, v_ref[g, pl.ds(ns, near * KB), :], preferred_element_type=F32)
                for a, (g, _, _), ns in zip(a_all, chains, nss)]
        if split_far:
            tops = [jnp.dot(a, v_ref[g, pl.ds(ws, KB), :], preferred_element_type=F32)
                    for a, (g, _, _), ws in zip(a_far, chains, wss)]
            accs = [jnp.concatenate([acc[:far0] + top, acc[far0:]], axis=0) for acc, top in zip(accs, tops)]
        return list(zip(carries, accs))

    def sweep_rest(g, r0, kb, carry, acc):
        q = q_ref[g, pl.ds(r0, QB), :]

        def cond(st):
            kb, _, _, mx = st
            return jnp.logical_and(kb >= 0, mx > LOG2_ZERO_F32)

        def body(st):
            kb, carry, acc, _ = st
            k0 = pl.multiple_of(kb * KB, KB)
            z = scores(q, k_ref[g, pl.ds(k0, KB), :])
            p = jnp.dot(_split_hi_lo(_softplus2(z)), mm_ref[...], preferred_element_type=F32)
            a = jnp.exp2(z + p[:, :KB] + carry)
            acc = acc + jnp.dot(a.astype(BF16), v_ref[g, pl.ds(k0, KB), :], preferred_element_type=F32)
            carry = carry + p[:, KB:]
            return kb - 1, carry, acc, jnp.max(carry)

        _, _, acc, _ = lax.while_loop(cond, body, (kb, carry, acc, jnp.max(carry)))
        return acc

    def qblocks(ils, nblk, far_rows=QB):
        chains = []
        for il in ils:
            r0 = il * QB if isinstance(il, int) else pl.multiple_of(il * QB, QB)
            chains += [(g, r0, first_block + il) for g in range(n_heads)]
        tri = causal_mask()
        res = windows(chains, tri, nblk, far_rows)
        live = jnp.max(functools.reduce(jnp.maximum, [c for c, _ in res])) > LOG2_ZERO_F32

        def exact():
            full = res if far_rows == QB else windows(chains, tri, nblk)
            return [sweep_rest(g, r0, ib - nblk, *rs) for (g, r0, ib), rs in zip(chains, full)]

        accs = lax.cond(live, exact, lambda: [a for _, a in res])
        for (g, r0, _), acc in zip(chains, accs):
            gated = acc * sg_ref[g, pl.ds(r0, QB), :]
            o_ref[pl.ds(r0, QB), g * dh:(g + 1) * dh] = gated.astype(o_ref.dtype)

    @pl.when(first_block == 0)
    def _():
        for il in range(WIN - 1):
            qblocks([il], il + 1)

    def body(it, _):
        qblocks([it * Q_BLOCKS_PER_ITER + j for j in range(Q_BLOCKS_PER_ITER)], WIN, FAR_ROWS)
        return 0

    assert (WIN - 1) % Q_BLOCKS_PER_ITER == 0 and nqb % Q_BLOCKS_PER_ITER == 0
    lax.fori_loop(jnp.where(first_block == 0, (WIN - 1) // Q_BLOCKS_PER_ITER, 0),
                  nqb // Q_BLOCKS_PER_ITER, body, 0)


def _attention(q, k, v, sg, heads=HEADS_PER_STEP, sq=Q_ROWS_PER_STEP):
    h, s, dh = q.shape
    q_spec = pl.BlockSpec((heads, sq, dh), lambda hg, j: (hg, j, 0))
    kv_spec = pl.BlockSpec((heads, s, dh), lambda hg, j: (hg, 0, 0))
    block_bytes = heads * dh * (sq * (2 + 4 + 2) + 2 * s * 2)
    return pl.pallas_call(
        _attn_kernel,
        grid=(h // heads, s // sq),
        in_specs=[q_spec, kv_spec, kv_spec, q_spec],
        out_specs=pl.BlockSpec((sq, heads * dh), lambda hg, j: (j, hg)),
        out_shape=jax.ShapeDtypeStruct((s, h * dh), BF16),
        scratch_shapes=[pltpu.VMEM((2 * KB, 2 * KB), BF16)],
        compiler_params=pltpu.CompilerParams(
            dimension_semantics=("arbitrary", "arbitrary"),
            vmem_limit_bytes=_vmem_limit(block_bytes, 4 * KB * KB * 2, 8 << 20)),
        name="stickbreak_attn",
    )(q, k, v, sg)


def _outproj_kernel(a_ref, w_ref, res_ref, gain_ref, h_ref, hg_ref, ssq_ref):
    half = w_ref.shape[1] // 2
    part = None
    for j in range(2):
        cols = slice(j * half, (j + 1) * half)
        h = res_ref[:, cols] + jnp.dot(a_ref[...], w_ref[:, cols], preferred_element_type=F32)
        h_ref[:, cols] = h
        hg_ref[:, cols] = (h * gain_ref[:, cols]).astype(hg_ref.dtype)
        hh = h * h
        for g in range(half // LANES):
            lanes = hh[:, g * LANES:(g + 1) * LANES]
            part = lanes if part is None else part + lanes

    @pl.when(pl.program_id(1) == 0)
    def _():
        ssq_ref[...] = part

    @pl.when(pl.program_id(1) > 0)
    def _():
        ssq_ref[...] += part


def _outproj_residual(a, w_bf16, res, gain, tm=1024, tn=512):
    s, kdim = a.shape
    n = w_bf16.shape[1]
    block_bytes = tm * kdim * 2 + kdim * tn * 2 + tm * tn * (4 + 4 + 2) + tn * 4 + tm * LANES * 4
    tile = pl.BlockSpec((tm, tn), lambda m, j: (m, j))
    return pl.pallas_call(
        _outproj_kernel,
        grid=(s // tm, n // tn),
        in_specs=[pl.BlockSpec((tm, kdim), lambda m, j: (m, 0)),
                  pl.BlockSpec((kdim, tn), lambda m, j: (0, j)),
                  tile,
                  pl.BlockSpec((1, tn), lambda m, j: (0, j))],
        out_specs=[tile, tile, pl.BlockSpec((tm, LANES), lambda m, j: (m, 0))],
        out_shape=[jax.ShapeDtypeStruct((s, n), F32), jax.ShapeDtypeStruct((s, n), BF16),
                   jax.ShapeDtypeStruct((s, LANES), F32)],
        compiler_params=pltpu.CompilerParams(
            dimension_semantics=("arbitrary", "arbitrary"),
            vmem_limit_bytes=_vmem_limit(block_bytes, 0, 3 * tm * tn * 4)),
        name="attn_outproj",
    )(a, w_bf16, res, gain)


def _outproj_norm_kernel(a_ref, w_ref, res_ref, gain_ref, o_ref):
    n = w_ref.shape[1]
    half = n // 2
    ssq = None
    for j in range(2):
        cols = slice(j * half, (j + 1) * half)
        h = res_ref[:, cols] + jnp.dot(a_ref[...], w_ref[:, cols], preferred_element_type=F32)
        o_ref[:, cols] = h
        part = jnp.sum(h * h, axis=1, keepdims=True)
        ssq = part if ssq is None else ssq + part
    rs = lax.rsqrt(ssq * (1.0 / n) + RMS_EPS)
    o_ref[...] = o_ref[...] * rs * gain_ref[...]


def _outproj_residual_norm(a, w_bf16, res, gain, tm=256):
    s, kdim = a.shape
    n = w_bf16.shape[1]
    rows = lambda width: pl.BlockSpec((tm, width), lambda m: (m, 0))
    w_spec = pl.BlockSpec((kdim, n), lambda m: (0, 0), pipeline_mode=pl.Buffered(1))
    need = kdim * n * 2 + 2 * (tm * kdim * 2 + 2 * tm * n * 4 + n * 4) + 2 * tm * n * 4 + (2 << 20)
    return pl.pallas_call(
        _outproj_norm_kernel,
        grid=(s // tm,),
        in_specs=[rows(kdim), w_spec, rows(n), pl.BlockSpec((1, n), lambda m: (0, 0))],
        out_specs=rows(n),
        out_shape=jax.ShapeDtypeStruct((s, n), F32),
        compiler_params=pltpu.CompilerParams(
            dimension_semantics=("arbitrary",),
            vmem_limit_bytes=int(min(need, VMEM_BYTES_V7X - (4 << 20)))),
        name="conv_outproj_norm",
    )(a, w_bf16, res, gain)


def kernel(x, norm_attn, w_in_attn, w_out_attn, norm_conv, w_in_conv, conv_w, w_out_conv, final_norm):
    b, s, d = x.shape
    assert b == 1 and d == N_HEADS * HEAD_DIM
    x2 = x.reshape(s, d)
    row = lambda g: g.reshape(1, d).astype(F32)

    q, k, v, sg, wo_attn_bf, wo_conv_bf = _attn_inproj(
        _rmsnorm_bf16(x2, row(norm_attn)), w_in_attn, w_out_attn, w_out_conv)
    og = _attention(q, k, v, sg)
    h1, h1g, ssq = _outproj_residual(og, wo_attn_bf, x2, row(norm_conv))
    yg = _conv_inproj(h1g, ssq, w_in_conv, conv_w.astype(F32))
    out = _outproj_residual_norm(yg, wo_conv_bf, h1, row(final_norm))
    return out.reshape(b, s, d)
```

```python
import functools
import math

import jax
import jax.numpy as jnp
from jax import lax
from jax.experimental import pallas as pl
from jax.experimental.pallas import tpu as pltpu

F32 = jnp.float32
BF16 = jnp.bfloat16

RMS_EPS = 1e-6
N_HEADS = 32
HEAD_DIM = 128
CONV_K = 3

LANES = 128
SUBLANES = 8
VMEM_BYTES_V7X = 64 * 1024 * 1024

QB = 128
KB = 128
WIN = 3
HEADS_PER_STEP = 4
Q_ROWS_PER_STEP = 2048
Q_BLOCKS_PER_ITER = 2
FAR_ROWS = QB // 2
LOG2_E = math.log2(math.e)
SCORE_SCALE2 = LOG2_E / math.sqrt(HEAD_DIM)
LOG2_ZERO_F32 = -152.0

NORM_ROWS = 16
NORM_UNROLL = 4


def _vmem_limit(block_bytes, scratch_bytes, temp_bytes):
    need = 2 * block_bytes + scratch_bytes + temp_bytes + (4 << 20)
    return int(min(need, VMEM_BYTES_V7X - (4 << 20)))


def _silu(g):
    return g * (1.0 / (1.0 + jnp.exp(-g)))


def _rmsnorm_rows(x_ref, gain_ref, dst_ref, rows):
    group = NORM_ROWS * NORM_UNROLL

    def body(r, _):
        base = r * group
        sls = [pl.ds(pl.multiple_of(base + j * NORM_ROWS, NORM_ROWS), NORM_ROWS) for j in range(NORM_UNROLL)]
        inv = []
        for sl in sls:
            xf = x_ref[sl, :]
            inv.append(lax.rsqrt(jnp.mean(xf * xf, axis=-1, keepdims=True) + RMS_EPS))
        for sl, rs in zip(sls, inv):
            dst_ref[sl, :] = (x_ref[sl, :] * rs * gain_ref[...]).astype(dst_ref.dtype)
        return 0
    lax.fori_loop(0, rows // group, body, 0)


def _norm_kernel(x_ref, gain_ref, o_ref):
    _rmsnorm_rows(x_ref, gain_ref, o_ref, x_ref.shape[0])


def _rmsnorm_bf16(x, gain, tm=1024):
    s, d = x.shape
    return pl.pallas_call(
        _norm_kernel,
        grid=(s // tm,),
        in_specs=[pl.BlockSpec((tm, d), lambda m: (m, 0)), pl.BlockSpec((1, d), lambda m: (0, 0))],
        out_specs=pl.BlockSpec((tm, d), lambda m: (m, 0)),
        out_shape=jax.ShapeDtypeStruct((s, d), BF16),
        compiler_params=pltpu.CompilerParams(
            dimension_semantics=("arbitrary",),
            vmem_limit_bytes=_vmem_limit(tm * d * (4 + 2) + d * 4, 0, 4 << 20)),
        name="rmsnorm_bf16",
    )(x, gain)


N_PROJ = 4
CAST_ROWS = 256


def _stage_weights(w_hbm, stage_ref, wbf_ref, sem, c, nblk):
    tn = stage_ref.shape[2]
    width = nblk * tn

    def copy(j, cc):
        col = pl.multiple_of(j * width + cc * tn, tn)
        return pltpu.make_async_copy(w_hbm.at[:, pl.ds(col, tn)], stage_ref.at[j], sem.at[j])

    @pl.when(c == 0)
    def _():
        for j in range(N_PROJ):
            copy(j, c).start()

    for j in range(N_PROJ):
        copy(j, c).wait()

    def cast(r, _):
        rows = pl.ds(pl.multiple_of(r * CAST_ROWS, CAST_ROWS), CAST_ROWS)
        for j in range(N_PROJ):
            wbf_ref[j, rows, :] = stage_ref[j, rows, :].astype(BF16)
        return 0
    lax.fori_loop(0, stage_ref.shape[1] // CAST_ROWS, cast, 0)

    @pl.when(c + 1 < nblk)
    def _():
        for j in range(N_PROJ):
            copy(j, c + 1).start()


def _proj(xn_ref, wbf_ref, j):
    return jnp.dot(xn_ref[...], wbf_ref[j], preferred_element_type=F32)


def _row_rsqrt(ssq_ref, d):
    return lax.rsqrt(jnp.sum(ssq_ref[...], axis=1, keepdims=True) * (1.0 / d) + RMS_EPS)


def _attn_inproj_kernel(xn_ref, w_hbm, wo_attn_ref, wo_conv_ref, q_ref, k_ref, v_ref, sg_ref,
                        wo_attn_bf_ref, wo_conv_bf_ref, stage_ref, wbf_ref, sem):
    @pl.when(pl.program_id(1) == 0)
    def _():
        _stage_weights(w_hbm, stage_ref, wbf_ref, sem, pl.program_id(0), pl.num_programs(0))

    heads =[slice(hh * HEAD_DIM, (hh + 1) * HEAD_DIM) for hh in range(q_ref.shape[0])]
    sg = _silu(_proj(xn_ref, wbf_ref, 3))
    for hh, cols in enumerate(heads):
        sg_ref[hh] = sg[:, cols]
    for j, (o_ref, mult) in enumerate(((q_ref, SCORE_SCALE2), (k_ref, None), (v_ref, None))):
        p = _proj(xn_ref, wbf_ref, j)
        if mult is not None:
            p = p * mult
        for hh, cols in enumerate(heads):
            o_ref[hh] = p[:, cols].astype(BF16)

    wo_attn_bf_ref[...] = wo_attn_ref[...].astype(BF16)
    wo_conv_bf_ref[...] = wo_conv_ref[...].astype(BF16)


def _conv_inproj_kernel(hg_ref, ssq_ref, w_hbm, cw_ref, y_ref, stage_ref, wbf_ref, sem, halo_ref):
    tm, d = hg_ref.shape
    m = pl.program_id(1)
    xn_ref = hg_ref

    @pl.when(m == 0)
    def _():
        _stage_weights(w_hbm, stage_ref, wbf_ref, sem, pl.program_id(0), pl.num_programs(0))

    rs = _row_rsqrt(ssq_ref, d)
    sg = _silu(rs * _proj(xn_ref, wbf_ref, 3))
    cu = (rs * _proj(xn_ref, wbf_ref, 1)) * (rs * _proj(xn_ref, wbf_ref, 2))

    halo = jnp.where(m > 0, halo_ref[...], 0.0)
    prev1 = halo[SUBLANES - 1:SUBLANES, :]
    prev2 = halo[SUBLANES - 2:SUBLANES - 1, :]
    row = lax.broadcasted_iota(jnp.int32, cu.shape, 0)
    cu1 = jnp.where(row == 0, prev1, pltpu.roll(cu, 1, 0))
    cu2 = jnp.where(row == 0, prev2, jnp.where(row == 1, prev1, pltpu.roll(cu, 2, 0)))
    halo_ref[...] = cu[tm - SUBLANES:, :]

    cw = cw_ref[...]
    conv = cw[0:1, :] * cu2 + cw[1:2, :] * cu1 + cw[2:3, :] * cu
    gb = rs * _proj(xn_ref, wbf_ref, 0)
    y_ref[...] = ((gb * conv) * sg).astype(y_ref.dtype)


def _inproj_call(kernel_fn, name, xn, w, extra_in, pre_in, pre_specs, extra_specs, out_specs, out_shape,
                 out_block_bytes, extra_scratch, tm, tn):
    s, d = xn.shape
    nblk = w.shape[1] // N_PROJ // tn
    x_spec = pl.BlockSpec((tm, d), lambda c, m: (m, 0))
    w_spec = pl.BlockSpec(memory_space=pl.ANY)
    scratch = [pltpu.VMEM((N_PROJ, d, tn), F32), pltpu.VMEM((N_PROJ, d, tn), BF16),
               pltpu.SemaphoreType.DMA((N_PROJ,))] + extra_scratch
    scratch_bytes = N_PROJ * d * tn * (4 + 2)
    temp_bytes = 8 * tm * tn * 4
    return pl.pallas_call(
        kernel_fn,
        grid=(nblk, s // tm),
        in_specs=[x_spec] + pre_specs + [w_spec] + extra_specs,
        out_specs=out_specs,
        out_shape=out_shape,
        scratch_shapes=scratch,
        compiler_params=pltpu.CompilerParams(
            dimension_semantics=("arbitrary", "arbitrary"),
            vmem_limit_bytes=_vmem_limit(tm * d * 2 + out_block_bytes, scratch_bytes, temp_bytes)),
        name=name,
    )(xn, *pre_in, w, *extra_in)


def _attn_inproj(xn, w, w_out_attn, w_out_conv, tm=1024, tn=256):
    s = xn.shape[0]
    hpt = tn // HEAD_DIM
    head_spec = pl.BlockSpec((hpt, tm, HEAD_DIM), lambda c, m: (c, m, 0))
    nm = s // tm
    kdim, n = w_out_attn.shape
    assert w_out_conv.shape == (kdim, n)
    slab = kdim // ((w.shape[1] // N_PROJ // tn) * nm)
    slab_spec = pl.BlockSpec((slab, n), lambda c, m: (c * nm + m, 0))
    return _inproj_call(
        _attn_inproj_kernel, "attn_inproj", xn, w, [w_out_attn, w_out_conv], [], [], [slab_spec] * 2,
        [head_spec] * 4 + [slab_spec] * 2,
        [jax.ShapeDtypeStruct((N_HEADS, s, HEAD_DIM), BF16)] * 3
        + [jax.ShapeDtypeStruct((N_HEADS, s, HEAD_DIM), F32)]
        + [jax.ShapeDtypeStruct((kdim, n), BF16)] * 2,
        hpt * tm * HEAD_DIM * (3 * 2 + 4) + 2 * slab * n * (4 + 2), [], tm, tn)


def _conv_inproj(hg, ssq, w, conv_w, tm=1024, tn=256):
    s = hg.shape[0]
    width = w.shape[1] // N_PROJ
    return _inproj_call(
        _conv_inproj_kernel, "conv_inproj", hg, w, [conv_w], [ssq],
        [pl.BlockSpec((tm, LANES), lambda c, m: (m, 0))],
        [pl.BlockSpec((CONV_K, tn), lambda c, m: (0, c))],
        pl.BlockSpec((tm, tn), lambda c, m: (m, c)),
        jax.ShapeDtypeStruct((s, width), BF16),
        tm * tn * 2, [pltpu.VMEM((SUBLANES, tn), F32)], tm, tn)


def _softplus2(z2):
    return jnp.maximum(z2, 0.0) + jnp.log(1.0 + jnp.exp2(-jnp.abs(z2))) * LOG2_E


def _split_hi_lo(x):
    hi = x.astype(BF16)
    lo = (x - hi.astype(F32)).astype(BF16)
    return jnp.concatenate([hi, lo], axis=1)


def _attn_kernel(q_ref, k_ref, v_ref, sg_ref, o_ref, mm_ref):
    n_heads, sq, dh = q_ref.shape
    nqb = sq // QB
    first_block = pl.program_id(1) * nqb
    nt = (((1,), (1,)), ((), ()))

    r = lax.broadcasted_iota(jnp.int32, (2 * KB, 2 * KB), 0) & (KB - 1)
    cc = lax.broadcasted_iota(jnp.int32, (2 * KB, 2 * KB), 1)
    mm_ref[...] = jnp.where((cc >= KB) | (r >= cc), -1.0, 0.0).astype(BF16)

    def scores(q, kblk):
        return lax.dot_general(q, kblk, nt, preferred_element_type=F32)

    def causal_mask():
        row = lax.broadcasted_iota(jnp.int32, (QB, KB), 0)
        col = lax.broadcasted_iota(jnp.int32, (QB, KB), 1)
        return col < row

    def windows(chains, tri, nblk, far_rows=QB):
        split_far = far_rows < QB
        near = nblk - 1 if split_far else nblk
        far0 = far_rows if split_far else 0
        wss = [pl.multiple_of((ib - (nblk - 1)) * KB, KB) for _, _, ib in chains]
        nss = [pl.multiple_of((ib - (near - 1)) * KB, KB) for _, _, ib in chains]
        zs = [scores(q_ref[g, pl.ds(r0, QB), :], k_ref[g, pl.ds(ns, near * KB), :])
              for (g, r0, _), ns in zip(chains, nss)]
        zfs = [scores(q_ref[g, pl.ds(r0, far_rows), :], k_ref[g, pl.ds(ws, KB), :])
               for (g, r0, _), ws in zip(chains, wss)] if split_far else [None] * len(chains)
        lhs = []
        for z, zf in zip(zs, zfs):
            sp = _softplus2(z)
            sps = [sp[:, b * KB:(b + 1) * KB] for b in range(near)]
            sps[-1] = jnp.where(tri, sps[-1], 0.0)
            if split_far:
                sps = [_softplus2(zf)] + sps
            lhs.append(jnp.concatenate([_split_hi_lo(s) for s in sps], axis=0))
        ps = [jnp.dot(l, mm_ref[...], preferred_element_type=F32) for l in lhs]
        a_all, a_far, carries = [], [], []
        for z, zf, p in zip(zs, zfs, ps):
            carry = None
            a_blocks = [None] * near
            for b in reversed(range(near)):
                pb = p[far0 + b * QB:far0 + (b + 1) * QB]
                log2_a = z[:, b * KB:(b + 1) * KB] + pb[:, :KB]
                if carry is not None:
                    log2_a = log2_a + carry
                a = jnp.exp2(log2_a)
                if b == near - 1:
                    a = jnp.where(tri, a, 0.0)
                a_blocks[b] = a.astype(BF16)
                carry = pb[:, KB:] if carry is None else carry + pb[:, KB:]
            if split_far:
                pf = p[:far0]
                a_far.append(jnp.exp2(zf + pf[:, :KB] + carry[:far0]).astype(BF16))
                carry = jnp.concatenate([carry[:far0] + pf[:, KB:], carry[far0:]], axis=0)
            a_all.append(jnp.concatenate(a_blocks, axis=1))
            carries.append(carry)
        accs = [jnp.dot(a, v_ref[g, pl.ds(ns, near * KB), :], preferred_element_type=F32)
                for a, (g, _, _), ns in zip(a_all, chains, nss)]
        if split_far:
            tops = [jnp.dot(a, v_ref[g, pl.ds(ws, KB), :], preferred_element_type=F32)
                    for a, (g, _, _), ws in zip(a_far, chains, wss)]
            accs = [jnp.concatenate([acc[:far0] + top, acc[far0:]], axis=0) for acc, top in zip(accs, tops)]
        return list(zip(carries, accs))

    def sweep_rest(g, r0, kb, carry, acc):
        q = q_ref[g, pl.ds(r0, QB), :]

        def cond(st):
            kb, _, _, mx = st
            return jnp.logical_and(kb >= 0, mx > LOG2_ZERO_F32)

        def body(st):
            kb, carry, acc, _ = st
            k0 = pl.multiple_of(kb * KB, KB)
            z = scores(q, k_ref[g, pl.ds(k0, KB), :])
            p = jnp.dot(_split_hi_lo(_softplus2(z)), mm_ref[...], preferred_element_type=F32)
            a = jnp.exp2(z + p[:, :KB] + carry)
            acc = acc + jnp.dot(a.astype(BF16), v_ref[g, pl.ds(k0, KB), :], preferred_element_type=F32)
            carry = carry + p[:, KB:]
            return kb - 1, carry, acc, jnp.max(carry)

        _, _, acc, _ = lax.while_loop(cond, body, (kb, carry, acc, jnp.max(carry)))
        return acc

    def qblocks(ils, nblk, far_rows=QB):
        chains = []
        for il in ils:
            r0 = il * QB if isinstance(il, int) else pl.multiple_of(il * QB, QB)
            chains += [(g, r0, first_block + il) for g in range(n_heads)]
        tri = causal_mask()
        res = windows(chains, tri, nblk, far_rows)
        live = jnp.max(functools.reduce(jnp.maximum, [c for c, _ in res])) > LOG2_ZERO_F32

        def exact():
            full = res if far_rows == QB else windows(chains, tri, nblk)
            return [sweep_rest(g, r0, ib - nblk, *rs) for (g, r0, ib), rs in zip(chains, full)]

        accs = lax.cond(live, exact, lambda: [a for _, a in res])
        for (g, r0, _), acc in zip(chains, accs):
            gated = acc * sg_ref[g, pl.ds(r0, QB), :]
            o_ref[pl.ds(r0, QB), g * dh:(g + 1) * dh] = gated.astype(o_ref.dtype)

    @pl.when(first_block == 0)
    def _():
        for il in range(WIN - 1):
            qblocks([il], il + 1)

    def body(it, _):
        qblocks([it * Q_BLOCKS_PER_ITER + j for j in range(Q_BLOCKS_PER_ITER)], WIN, FAR_ROWS)
        return 0

    assert (WIN - 1) % Q_BLOCKS_PER_ITER == 0 and nqb % Q_BLOCKS_PER_ITER == 0
    lax.fori_loop(jnp.where(first_block == 0, (WIN - 1) // Q_BLOCKS_PER_ITER, 0),
                  nqb // Q_BLOCKS_PER_ITER, body, 0)


def _attention(q, k, v, sg, heads=HEADS_PER_STEP, sq=Q_ROWS_PER_STEP):
    h, s, dh = q.shape
    q_spec = pl.BlockSpec((heads, sq, dh), lambda hg, j: (hg, j, 0))
    kv_spec = pl.BlockSpec((heads, s, dh), lambda hg, j: (hg, 0, 0))
    block_bytes = heads * dh * (sq * (2 + 4 + 2) + 2 * s * 2)
    return pl.pallas_call(
        _attn_kernel,
        grid=(h // heads, s // sq),
        in_specs=[q_spec, kv_spec, kv_spec, q_spec],
        out_specs=pl.BlockSpec((sq, heads * dh), lambda hg, j: (j, hg)),
        out_shape=jax.ShapeDtypeStruct((s, h * dh), BF16),
        scratch_shapes=[pltpu.VMEM((2 * KB, 2 * KB), BF16)],
        compiler_params=pltpu.CompilerParams(
            dimension_semantics=("arbitrary", "arbitrary"),
            vmem_limit_bytes=_vmem_limit(block_bytes, 4 * KB * KB * 2, 8 << 20)),
        name="stickbreak_attn",
    )(q, k, v, sg)


def _outproj_kernel(a_ref, w_ref, res_ref, gain_ref, h_ref, hg_ref, ssq_ref):
    half = w_ref.shape[1] // 2
    part = None
    for j in range(2):
        cols = slice(j * half, (j + 1) * half)
        h = res_ref[:, cols] + jnp.dot(a_ref[...], w_ref[:, cols], preferred_element_type=F32)
        h_ref[:, cols] = h
        hg_ref[:, cols] = (h * gain_ref[:, cols]).astype(hg_ref.dtype)
        hh = h * h
        for g in range(half // LANES):
            lanes = hh[:, g * LANES:(g + 1) * LANES]
            part = lanes if part is None else part + lanes

    @pl.when(pl.program_id(1) == 0)
    def _():
        ssq_ref[...] = part

    @pl.when(pl.program_id(1) > 0)
    def _():
        ssq_ref[...] += part


def _outproj_residual(a, w_bf16, res, gain, tm=1024, tn=1024):
    s, kdim = a.shape
    n = w_bf16.shape[1]
    block_bytes = tm * kdim * 2 + kdim * tn * 2 + tm * tn * (4 + 4 + 2) + tn * 4 + tm * LANES * 4
    tile = pl.BlockSpec((tm, tn), lambda m, j: (m, j))
    return pl.pallas_call(
        _outproj_kernel,
        grid=(s // tm, n // tn),
        in_specs=[pl.BlockSpec((tm, kdim), lambda m, j: (m, 0)),
                  pl.BlockSpec((kdim, tn), lambda m, j: (0, j)),
                  tile,
                  pl.BlockSpec((1, tn), lambda m, j: (0, j))],
        out_specs=[tile, tile, pl.BlockSpec((tm, LANES), lambda m, j: (m, 0))],
        out_shape=[jax.ShapeDtypeStruct((s, n), F32), jax.ShapeDtypeStruct((s, n), BF16),
                   jax.ShapeDtypeStruct((s, LANES), F32)],
        compiler_params=pltpu.CompilerParams(
            dimension_semantics=("arbitrary", "arbitrary"),
            vmem_limit_bytes=_vmem_limit(block_bytes, 0, 3 * tm * tn * 4)),
        name="attn_outproj",
    )(a, w_bf16, res, gain)


def _outproj_norm_kernel(a_ref, w_ref, res_ref, gain_ref, o_ref):
    n = w_ref.shape[1]
    half = n // 2
    ssq = None
    for j in range(2):
        cols = slice(j * half, (j + 1) * half)
        h = res_ref[:, cols] + jnp.dot(a_ref[...], w_ref[:, cols], preferred_element_type=F32)
        o_ref[:, cols] = h
        part = jnp.sum(h * h, axis=1, keepdims=True)
        ssq = part if ssq is None else ssq + part
    rs = lax.rsqrt(ssq * (1.0 / n) + RMS_EPS)
    o_ref[...] = o_ref[...] * rs * gain_ref[...]


def _outproj_residual_norm(a, w_bf16, res, gain, tm=256):
    s, kdim = a.shape
    n = w_bf16.shape[1]
    rows = lambda width: pl.BlockSpec((tm, width), lambda m: (m, 0))
    w_spec = pl.BlockSpec((kdim, n), lambda m: (0, 0), pipeline_mode=pl.Buffered(1))
    need = kdim * n * 2 + 2 * (tm * kdim * 2 + 2 * tm * n * 4 + n * 4) + 2 * tm * n * 4 + (2 << 20)
    return pl.pallas_call(
        _outproj_norm_kernel,
        grid=(s // tm,),
        in_specs=[rows(kdim), w_spec, rows(n), pl.BlockSpec((1, n), lambda m: (0, 0))],
        out_specs=rows(n),
        out_shape=jax.ShapeDtypeStruct((s, n), F32),
        compiler_params=pltpu.CompilerParams(
            dimension_semantics=("arbitrary",),
            vmem_limit_bytes=int(min(need, VMEM_BYTES_V7X - (4 << 20)))),
        name="conv_outproj_norm",
    )(a, w_bf16, res, gain)


def kernel(x, norm_attn, w_in_attn, w_out_attn, norm_conv, w_in_conv, conv_w, w_out_conv, final_norm):
    b, s, d = x.shape
    assert b == 1 and d == N_HEADS * HEAD_DIM
    x2 = x.reshape(s, d)
    row = lambda g: g.reshape(1, d).astype(F32)

    q, k, v, sg, wo_attn_bf, wo_conv_bf = _attn_inproj(
        _rmsnorm_bf16(x2, row(norm_attn)), w_in_attn, w_out_attn, w_out_conv)
    og = _attention(q, k, v, sg)
    h1, h1g, ssq = _outproj_residual(og, wo_attn_bf, x2, row(norm_conv))
    yg = _conv_inproj(h1g, ssq, w_in_conv, conv_w.astype(F32))
    out = _outproj_residual_norm(yg, wo_conv_bf, h1, row(final_norm))
    return out.reshape(b, s, d)
```

```python
import functools
import math

import jax
import jax.numpy as jnp
from jax import lax
from jax.experimental import pallas as pl
from jax.experimental.pallas import tpu as pltpu

F32 = jnp.float32
BF16 = jnp.bfloat16

RMS_EPS = 1e-6
N_HEADS = 32
HEAD_DIM = 128
CONV_K = 3

LANES = 128
SUBLANES = 8
VMEM_BYTES_V7X = 64 * 1024 * 1024

QB = 128
KB = 128
WIN = 3
HEADS_PER_STEP = 4
Q_ROWS_PER_STEP = 2048
Q_BLOCKS_PER_ITER = 2
FAR_ROWS = QB // 2
LOG2_E = math.log2(math.e)
SCORE_SCALE2 = LOG2_E / math.sqrt(HEAD_DIM)
LOG2_ZERO_F32 = -152.0

NORM_ROWS = 16
NORM_UNROLL = 4


VMEM_SPILL_ALLOWANCE = 4 << 20
VMEM_REQUEST_CAP = VMEM_BYTES_V7X - (4 << 20)


def _vmem_limit(block_bytes, scratch_bytes, temp_bytes):
    need = 2 * block_bytes + scratch_bytes + temp_bytes + VMEM_SPILL_ALLOWANCE
    return int(min(need, VMEM_REQUEST_CAP))


def _silu(g):
    return g * (1.0 / (1.0 + jnp.exp(-g)))


def _rmsnorm_rows(x_ref, gain_ref, dst_ref, rows):
    group = NORM_ROWS * NORM_UNROLL

    def body(r, _):
        base = r * group
        sls = [pl.ds(pl.multiple_of(base + j * NORM_ROWS, NORM_ROWS), NORM_ROWS) for j in range(NORM_UNROLL)]
        inv = []
        for sl in sls:
            xf = x_ref[sl, :]
            inv.append(lax.rsqrt(jnp.mean(xf * xf, axis=-1, keepdims=True) + RMS_EPS))
        for sl, rs in zip(sls, inv):
            dst_ref[sl, :] = (x_ref[sl, :] * rs * gain_ref[...]).astype(dst_ref.dtype)
        return 0
    lax.fori_loop(0, rows // group, body, 0)


def _norm_kernel(x_ref, gain_ref, o_ref):
    _rmsnorm_rows(x_ref, gain_ref, o_ref, x_ref.shape[0])


def _rmsnorm_bf16(x, gain, tm=1024):
    s, d = x.shape
    return pl.pallas_call(
        _norm_kernel,
        grid=(s // tm,),
        in_specs=[pl.BlockSpec((tm, d), lambda m: (m, 0)), pl.BlockSpec((1, d), lambda m: (0, 0))],
        out_specs=pl.BlockSpec((tm, d), lambda m: (m, 0)),
        out_shape=jax.ShapeDtypeStruct((s, d), BF16),
        compiler_params=pltpu.CompilerParams(
            dimension_semantics=("arbitrary",),
            vmem_limit_bytes=_vmem_limit(tm * d * (4 + 2) + d * 4, 0, 4 << 20)),
        name="rmsnorm_bf16",
    )(x, gain)


N_PROJ = 4
CAST_ROWS = 256


def _stage_weights(w_hbm, stage_ref, wbf_ref, sem, c, nblk):
    tn = stage_ref.shape[2]
    width = nblk * tn

    def copy(j, cc):
        col = pl.multiple_of(j * width + cc * tn, tn)
        return pltpu.make_async_copy(w_hbm.at[:, pl.ds(col, tn)], stage_ref.at[j], sem.at[j])

    @pl.when(c == 0)
    def _():
        for j in range(N_PROJ):
            copy(j, c).start()

    for j in range(N_PROJ):
        copy(j, c).wait()

    def cast(r, _):
        rows = pl.ds(pl.multiple_of(r * CAST_ROWS, CAST_ROWS), CAST_ROWS)
        for j in range(N_PROJ):
            wbf_ref[j, rows, :] = stage_ref[j, rows, :].astype(BF16)
        return 0
    lax.fori_loop(0, stage_ref.shape[1] // CAST_ROWS, cast, 0)

    @pl.when(c + 1 < nblk)
    def _():
        for j in range(N_PROJ):
            copy(j, c + 1).start()


def _proj(xn_ref, wbf_ref, j):
    return jnp.dot(xn_ref[...], wbf_ref[j], preferred_element_type=F32)


def _row_rsqrt(ssq_ref, d):
    return lax.rsqrt(jnp.sum(ssq_ref[...], axis=1, keepdims=True) * (1.0 / d) + RMS_EPS)


def _attn_inproj_kernel(xn_ref, w_hbm, wo_attn_ref, wo_conv_ref, q_ref, k_ref, v_ref, sg_ref,
                        wo_attn_bf_ref, wo_conv_bf_ref, stage_ref, wbf_ref, sem):
    @pl.when(pl.program_id(1) == 0)
    def _():
        _stage_weights(w_hbm, stage_ref, wbf_ref, sem, pl.program_id(0), pl.num_programs(0))

    heads =[slice(hh * HEAD_DIM, (hh + 1) * HEAD_DIM) for hh in range(q_ref.shape[0])]
    sg = _silu(_proj(xn_ref, wbf_ref, 3))
    for hh, cols in enumerate(heads):
        sg_ref[hh] = sg[:, cols]
    for j, (o_ref, mult) in enumerate(((q_ref, SCORE_SCALE2), (k_ref, None), (v_ref, None))):
        p = _proj(xn_ref, wbf_ref, j)
        if mult is not None:
            p = p * mult
        for hh, cols in enumerate(heads):
            o_ref[hh] = p[:, cols].astype(BF16)

    wo_attn_bf_ref[...] = wo_attn_ref[...].astype(BF16)
    wo_conv_bf_ref[...] = wo_conv_ref[...].astype(BF16)


def _conv_inproj_kernel(hg_ref, ssq_ref, w_hbm, cw_ref, y_ref, stage_ref, wbf_ref, sem, halo_ref):
    tm, d = hg_ref.shape
    m = pl.program_id(1)
    xn_ref = hg_ref

    @pl.when(m == 0)
    def _():
        _stage_weights(w_hbm, stage_ref, wbf_ref, sem, pl.program_id(0), pl.num_programs(0))

    rs = _row_rsqrt(ssq_ref, d)
    sgr = _silu(rs * _proj(xn_ref, wbf_ref, 3)) * rs
    cu = (rs * rs) * (_proj(xn_ref, wbf_ref, 1) * _proj(xn_ref, wbf_ref, 2))

    halo = jnp.where(m > 0, halo_ref[...], 0.0)
    prev1 = halo[SUBLANES - 1:SUBLANES, :]
    prev2 = halo[SUBLANES - 2:SUBLANES - 1, :]
    row = lax.broadcasted_iota(jnp.int32, cu.shape, 0)
    cu1 = jnp.where(row == 0, prev1, pltpu.roll(cu, 1, 0))
    cu2 = jnp.where(row == 0, prev2, jnp.where(row == 1, prev1, pltpu.roll(cu, 2, 0)))
    halo_ref[...] = cu[tm - SUBLANES:, :]

    cw = cw_ref[...]
    conv = cw[0:1, :] * cu2 + cw[1:2, :] * cu1 + cw[2:3, :] * cu
    y_ref[...] = (_proj(xn_ref, wbf_ref, 0) * (conv * sgr)).astype(y_ref.dtype)


def _inproj_call(kernel_fn, name, xn, w, extra_in, pre_in, pre_specs, extra_specs, out_specs, out_shape,
                 out_block_bytes, extra_scratch, tm, tn):
    s, d = xn.shape
    nblk = w.shape[1] // N_PROJ // tn
    x_spec = pl.BlockSpec((tm, d), lambda c, m: (m, 0))
    w_spec = pl.BlockSpec(memory_space=pl.ANY)
    scratch = [pltpu.VMEM((N_PROJ, d, tn), F32), pltpu.VMEM((N_PROJ, d, tn), BF16),
               pltpu.SemaphoreType.DMA((N_PROJ,))] + extra_scratch
    scratch_bytes = N_PROJ * d * tn * (4 + 2)
    temp_bytes = 8 * tm * tn * 4
    return pl.pallas_call(
        kernel_fn,
        grid=(nblk, s // tm),
        in_specs=[x_spec] + pre_specs + [w_spec] + extra_specs,
        out_specs=out_specs,
        out_shape=out_shape,
        scratch_shapes=scratch,
        compiler_params=pltpu.CompilerParams(
            dimension_semantics=("arbitrary", "arbitrary"),
            vmem_limit_bytes=_vmem_limit(tm * d * 2 + out_block_bytes, scratch_bytes, temp_bytes)),
        name=name,
    )(xn, *pre_in, w, *extra_in)


def _attn_inproj(xn, w, w_out_attn, w_out_conv, tm=1024, tn=256):
    s = xn.shape[0]
    hpt = tn // HEAD_DIM
    head_spec = pl.BlockSpec((hpt, tm, HEAD_DIM), lambda c, m: (c, m, 0))
    nm = s // tm
    kdim, n = w_out_attn.shape
    assert w_out_conv.shape == (kdim, n)
    slab = kdim // ((w.shape[1] // N_PROJ // tn) * nm)
    slab_spec = pl.BlockSpec((slab, n), lambda c, m: (c * nm + m, 0))
    return _inproj_call(
        _attn_inproj_kernel, "attn_inproj", xn, w, [w_out_attn, w_out_conv], [], [], [slab_spec] * 2,
        [head_spec] * 4 + [slab_spec] * 2,
        [jax.ShapeDtypeStruct((N_HEADS, s, HEAD_DIM), BF16)] * 3
        + [jax.ShapeDtypeStruct((N_HEADS, s, HEAD_DIM), F32)]
        + [jax.ShapeDtypeStruct((kdim, n), BF16)] * 2,
        hpt * tm * HEAD_DIM * (3 * 2 + 4) + 2 * slab * n * (4 + 2), [], tm, tn)


def _conv_inproj(hg, ssq, w, conv_w, tm=1024, tn=256):
    s = hg.shape[0]
    width = w.shape[1] // N_PROJ
    return _inproj_call(
        _conv_inproj_kernel, "conv_inproj", hg, w, [conv_w], [ssq],
        [pl.BlockSpec((tm, LANES), lambda c, m: (m, 0))],
        [pl.BlockSpec((CONV_K, tn), lambda c, m: (0, c))],
        pl.BlockSpec((tm, tn), lambda c, m: (m, c)),
        jax.ShapeDtypeStruct((s, width), BF16),
        tm * tn * 2, [pltpu.VMEM((SUBLANES, tn), F32)], tm, tn)


def _softplus2(z2):
    return jnp.maximum(z2, 0.0) + jnp.log(1.0 + jnp.exp2(-jnp.abs(z2))) * LOG2_E


def _split_hi_lo(x):
    hi = x.astype(BF16)
    lo = (x - hi.astype(F32)).astype(BF16)
    return jnp.concatenate([hi, lo], axis=1)


def _attn_kernel(q_ref, k_ref, v_ref, sg_ref, o_ref, mm_ref):
    n_heads, sq, dh = q_ref.shape
    nqb = sq // QB
    first_block = pl.program_id(1) * nqb
    nt = (((1,), (1,)), ((), ()))

    r = lax.broadcasted_iota(jnp.int32, (2 * KB, 2 * KB), 0) & (KB - 1)
    cc = lax.broadcasted_iota(jnp.int32, (2 * KB, 2 * KB), 1)
    mm_ref[...] = jnp.where((cc >= KB) | (r >= cc), -1.0, 0.0).astype(BF16)

    def scores(q, kblk):
        return lax.dot_general(q, kblk, nt, preferred_element_type=F32)

    def causal_mask():
        row = lax.broadcasted_iota(jnp.int32, (QB, KB), 0)
        col = lax.broadcasted_iota(jnp.int32, (QB, KB), 1)
        return col < row

    def windows(chains, tri, nblk, far_rows=QB):
        split_far = far_rows < QB
        near = nblk - 1 if split_far else nblk
        far0 = far_rows if split_far else 0
        wss = [pl.multiple_of((ib - (nblk - 1)) * KB, KB) for _, _, ib in chains]
        nss = [pl.multiple_of((ib - (near - 1)) * KB, KB) for _, _, ib in chains]
        zs = [scores(q_ref[g, pl.ds(r0, QB), :], k_ref[g, pl.ds(ns, near * KB), :])
              for (g, r0, _), ns in zip(chains, nss)]
        zfs = [scores(q_ref[g, pl.ds(r0, far_rows), :], k_ref[g, pl.ds(ws, KB), :])
               for (g, r0, _), ws in zip(chains, wss)] if split_far else [None] * len(chains)
        lhs = []
        for z, zf in zip(zs, zfs):
            sp = _softplus2(z)
            sps = [sp[:, b * KB:(b + 1) * KB] for b in range(near)]
            sps[-1] = jnp.where(tri, sps[-1], 0.0)
            if split_far:
                sps = [_softplus2(zf)] + sps
            lhs.append(jnp.concatenate([_split_hi_lo(s) for s in sps], axis=0))
        ps = [jnp.dot(l, mm_ref[...], preferred_element_type=F32) for l in lhs]
        a_all, a_far, carries = [], [], []
        for z, zf, p in zip(zs, zfs, ps):
            carry = None
            a_blocks = [None] * near
            for b in reversed(range(near)):
                pb = p[far0 + b * QB:far0 + (b + 1) * QB]
                log2_a = z[:, b * KB:(b + 1) * KB] + pb[:, :KB]
                if carry is not None:
                    log2_a = log2_a + carry
                a = jnp.exp2(log2_a)
                if b == near - 1:
                    a = jnp.where(tri, a, 0.0)
                a_blocks[b] = a.astype(BF16)
                carry = pb[:, KB:] if carry is None else carry + pb[:, KB:]
            if split_far:
                pf = p[:far0]
                a_far.append(jnp.exp2(zf + pf[:, :KB] + carry[:far0]).astype(BF16))
                carry = jnp.concatenate([carry[:far0] + pf[:, KB:], carry[far0:]], axis=0)
            a_all.append(jnp.concatenate(a_blocks, axis=1))
            carries.append(carry)
        accs = [jnp.dot(a, v_ref[g, pl.ds(ns, near * KB), :], preferred_element_type=F32)
                for a, (g, _, _), ns in zip(a_all, chains, nss)]
        if split_far:
            tops = [jnp.dot(a, v_ref[g, pl.ds(ws, KB), :], preferred_element_type=F32)
                    for a, (g, _, _), ws in zip(a_far, chains, wss)]
            accs = [jnp.concatenate([acc[:far0] + top, acc[far0:]], axis=0) for acc, top in zip(accs, tops)]
        return list(zip(carries, accs))

    def sweep_rest(g, r0, kb, carry, acc):
        q = q_ref[g, pl.ds(r0, QB), :]

        def cond(st):
            kb, _, _, mx = st
            return jnp.logical_and(kb >= 0, mx > LOG2_ZERO_F32)

        def body(st):
            kb, carry, acc, _ = st
            k0 = pl.multiple_of(kb * KB, KB)
            z = scores(q, k_ref[g, pl.ds(k0, KB), :])
            p = jnp.dot(_split_hi_lo(_softplus2(z)), mm_ref[...], preferred_element_type=F32)
            a = jnp.exp2(z + p[:, :KB] + carry)
            acc = acc + jnp.dot(a.astype(BF16), v_ref[g, pl.ds(k0, KB), :], preferred_element_type=F32)
            carry = carry + p[:, KB:]
            return kb - 1, carry, acc, jnp.max(carry)

        _, _, acc, _ = lax.while_loop(cond, body, (kb, carry, acc, jnp.max(carry)))
        return acc

    def qblocks(ils, nblk, far_rows=QB):
        chains = []
        for il in ils:
            r0 = il * QB if isinstance(il, int) else pl.multiple_of(il * QB, QB)
            chains += [(g, r0, first_block + il) for g in range(n_heads)]
        tri = causal_mask()
        res = windows(chains, tri, nblk, far_rows)
        live = jnp.max(functools.reduce(jnp.maximum, [c for c, _ in res])) > LOG2_ZERO_F32

        def exact():
            full = res if far_rows == QB else windows(chains, tri, nblk)
            return [sweep_rest(g, r0, ib - nblk, *rs) for (g, r0, ib), rs in zip(chains, full)]

        accs = lax.cond(live, exact, lambda: [a for _, a in res])
        for (g, r0, _), acc in zip(chains, accs):
            gated = acc * sg_ref[g, pl.ds(r0, QB), :]
            o_ref[pl.ds(r0, QB), g * dh:(g + 1) * dh] = gated.astype(o_ref.dtype)

    @pl.when(first_block == 0)
    def _():
        for il in range(WIN - 1):
            qblocks([il], il + 1)

    def body(it, _):
        qblocks([it * Q_BLOCKS_PER_ITER + j for j in range(Q_BLOCKS_PER_ITER)], WIN, FAR_ROWS)
        return 0

    assert (WIN - 1) % Q_BLOCKS_PER_ITER == 0 and nqb % Q_BLOCKS_PER_ITER == 0
    lax.fori_loop(jnp.where(first_block == 0, (WIN - 1) // Q_BLOCKS_PER_ITER, 0),
                  nqb // Q_BLOCKS_PER_ITER, body, 0)


def _attention(q, k, v, sg, heads=HEADS_PER_STEP, sq=Q_ROWS_PER_STEP):
    h, s, dh = q.shape
    q_spec = pl.BlockSpec((heads, sq, dh), lambda hg, j: (hg, j, 0))
    kv_spec = pl.BlockSpec((heads, s, dh), lambda hg, j: (hg, 0, 0))
    block_bytes = heads * dh * (sq * (2 + 4 + 2) + 2 * s * 2)
    return pl.pallas_call(
        _attn_kernel,
        grid=(h // heads, s // sq),
        in_specs=[q_spec, kv_spec, kv_spec, q_spec],
        out_specs=pl.BlockSpec((sq, heads * dh), lambda hg, j: (j, hg)),
        out_shape=jax.ShapeDtypeStruct((s, h * dh), BF16),
        scratch_shapes=[pltpu.VMEM((2 * KB, 2 * KB), BF16)],
        compiler_params=pltpu.CompilerParams(
            dimension_semantics=("arbitrary", "arbitrary"),
            vmem_limit_bytes=_vmem_limit(block_bytes, 4 * KB * KB * 2, 8 << 20)),
        name="stickbreak_attn",
    )(q, k, v, sg)


def _outproj_kernel(a_ref, w_ref, res_ref, gain_ref, h_ref, hg_ref, ssq_ref):
    half = w_ref.shape[1] // 2
    part = None
    for j in range(2):
        cols = slice(j * half, (j + 1) * half)
        h = res_ref[:, cols] + jnp.dot(a_ref[...], w_ref[:, cols], preferred_element_type=F32)
        h_ref[:, cols] = h
        hg_ref[:, cols] = (h * gain_ref[:, cols]).astype(hg_ref.dtype)
        hh = h * h
        for g in range(half // LANES):
            lanes = hh[:, g * LANES:(g + 1) * LANES]
            part = lanes if part is None else part + lanes

    @pl.when(pl.program_id(1) == 0)
    def _():
        ssq_ref[...] = part

    @pl.when(pl.program_id(1) > 0)
    def _():
        ssq_ref[...] += part


def _outproj_residual(a, w_bf16, res, gain, tm=1024, tn=1024):
    s, kdim = a.shape
    n = w_bf16.shape[1]
    block_bytes = tm * kdim * 2 + kdim * tn * 2 + tm * tn * (4 + 4 + 2) + tn * 4 + tm * LANES * 4
    tile = pl.BlockSpec((tm, tn), lambda m, j: (m, j))
    return pl.pallas_call(
        _outproj_kernel,
        grid=(s // tm, n // tn),
        in_specs=[pl.BlockSpec((tm, kdim), lambda m, j: (m, 0)),
                  pl.BlockSpec((kdim, tn), lambda m, j: (0, j)),
                  tile,
                  pl.BlockSpec((1, tn), lambda m, j: (0, j))],
        out_specs=[tile, tile, pl.BlockSpec((tm, LANES), lambda m, j: (m, 0))],
        out_shape=[jax.ShapeDtypeStruct((s, n), F32), jax.ShapeDtypeStruct((s, n), BF16),
                   jax.ShapeDtypeStruct((s, LANES), F32)],
        compiler_params=pltpu.CompilerParams(
            dimension_semantics=("arbitrary", "arbitrary"),
            vmem_limit_bytes=_vmem_limit(block_bytes, 0, 3 * tm * tn * 4)),
        name="attn_outproj",
    )(a, w_bf16, res, gain)


def _outproj_norm_kernel(a_ref, w_ref, res_ref, gain_ref, o_ref):
    n = w_ref.shape[1]
    half = n // 2
    ssq = None
    for j in range(2):
        cols = slice(j * half, (j + 1) * half)
        h = res_ref[:, cols] + jnp.dot(a_ref[...], w_ref[:, cols], preferred_element_type=F32)
        o_ref[:, cols] = h * gain_ref[:, cols]
        part = jnp.sum(h * h, axis=1, keepdims=True)
        ssq = part if ssq is None else ssq + part
    rs = lax.rsqrt(ssq * (1.0 / n) + RMS_EPS)
    o_ref[...] = o_ref[...] * rs


def _outproj_residual_norm(a, w_bf16, res, gain, tm=256):
    s, kdim = a.shape
    n = w_bf16.shape[1]
    rows = lambda width: pl.BlockSpec((tm, width), lambda m: (m, 0))
    w_spec = pl.BlockSpec((kdim, n), lambda m: (0, 0), pipeline_mode=pl.Buffered(1))
    need = kdim * n * 2 + 2 * (tm * kdim * 2 + 2 * tm * n * 4 + n * 4) + 2 * tm * n * 4 + VMEM_SPILL_ALLOWANCE // 2
    return pl.pallas_call(
        _outproj_norm_kernel,
        grid=(s // tm,),
        in_specs=[rows(kdim), w_spec, rows(n), pl.BlockSpec((1, n), lambda m: (0, 0))],
        out_specs=rows(n),
        out_shape=jax.ShapeDtypeStruct((s, n), F32),
        compiler_params=pltpu.CompilerParams(
            dimension_semantics=("arbitrary",),
            vmem_limit_bytes=int(min(need, VMEM_REQUEST_CAP))),
        name="conv_outproj_norm",
    )(a, w_bf16, res, gain)


def kernel(x, norm_attn, w_in_attn, w_out_attn, norm_conv, w_in_conv, conv_w, w_out_conv, final_norm):
    b, s, d = x.shape
    assert b == 1 and d == N_HEADS * HEAD_DIM
    x2 = x.reshape(s, d)
    row = lambda g: g.reshape(1, d).astype(F32)

    q, k, v, sg, wo_attn_bf, wo_conv_bf = _attn_inproj(
        _rmsnorm_bf16(x2, row(norm_attn)), w_in_attn, w_out_attn, w_out_conv)
    og = _attention(q, k, v, sg)
    h1, h1g, ssq = _outproj_residual(og, wo_attn_bf, x2, row(norm_conv))
    yg = _conv_inproj(h1g, ssq, w_in_conv, conv_w.astype(F32))
    out = _outproj_residual_norm(yg, wo_conv_bf, h1, row(final_norm))
    return out.reshape(b, s, d)
```

```python
import functools
import math

import jax
import jax.numpy as jnp
from jax import lax
from jax.experimental import pallas as pl
from jax.experimental.pallas import tpu as pltpu

F32 = jnp.float32
BF16 = jnp.bfloat16

RMS_EPS = 1e-6
N_HEADS = 32
HEAD_DIM = 128
CONV_K = 3

LANES = 128
SUBLANES = 8
VMEM_BYTES_V7X = 64 * 1024 * 1024

QB = 128
KB = 128
WIN = 3
HEADS_PER_STEP = 4
Q_ROWS_PER_STEP = 2048
Q_BLOCKS_PER_ITER = 2
FAR_ROWS = QB // 2
LOG2_E = math.log2(math.e)
SCORE_SCALE2 = LOG2_E / math.sqrt(HEAD_DIM)
LOG2_ZERO_F32 = -152.0

NORM_ROWS = 16
NORM_UNROLL = 4


def _vmem_limit(block_bytes, scratch_bytes, temp_bytes):
    need = 2 * block_bytes + scratch_bytes + temp_bytes + (4 << 20)
    return int(min(need, VMEM_BYTES_V7X - (4 << 20)))


def _silu(g):
    return g * (1.0 / (1.0 + jnp.exp(-g)))


def _rmsnorm_rows(x_ref, gain_ref, dst_ref, rows):
    group = NORM_ROWS * NORM_UNROLL

    def body(r, _):
        base = r * group
        sls = [pl.ds(pl.multiple_of(base + j * NORM_ROWS, NORM_ROWS), NORM_ROWS) for j in range(NORM_UNROLL)]
        inv = []
        for sl in sls:
            xf = x_ref[sl, :]
            inv.append(lax.rsqrt(jnp.mean(xf * xf, axis=-1, keepdims=True) + RMS_EPS))
        for sl, rs in zip(sls, inv):
            dst_ref[sl, :] = (x_ref[sl, :] * rs * gain_ref[...]).astype(dst_ref.dtype)
        return 0
    lax.fori_loop(0, rows // group, body, 0)


def _norm_kernel(x_ref, gain_ref, o_ref):
    _rmsnorm_rows(x_ref, gain_ref, o_ref, x_ref.shape[0])


def _rmsnorm_bf16(x, gain, tm=1024):
    s, d = x.shape
    return pl.pallas_call(
        _norm_kernel,
        grid=(s // tm,),
        in_specs=[pl.BlockSpec((tm, d), lambda m: (m, 0)), pl.BlockSpec((1, d), lambda m: (0, 0))],
        out_specs=pl.BlockSpec((tm, d), lambda m: (m, 0)),
        out_shape=jax.ShapeDtypeStruct((s, d), BF16),
        compiler_params=pltpu.CompilerParams(
            dimension_semantics=("arbitrary",),
            vmem_limit_bytes=_vmem_limit(tm * d * (4 + 2) + d * 4, 0, 4 << 20)),
        name="rmsnorm_bf16",
    )(x, gain)


N_PROJ = 4
CAST_ROWS = 256


def _run_with_staged_weights(body, order, xn_ref, w_hbm, stage_ref, wbf_ref, sem):
    c, nblk, m = pl.program_id(0), pl.num_programs(0), pl.program_id(1)
    tn = stage_ref.shape[2]
    width = nblk * tn

    def copy(j, cc):
        col = pl.multiple_of(j * width + cc * tn, tn)
        return pltpu.make_async_copy(w_hbm.at[:, pl.ds(col, tn)], stage_ref.at[j], sem.at[j])

    def dot(j):
        return jnp.dot(xn_ref[...], wbf_ref[j], preferred_element_type=F32)

    def cast(j):
        wbf_ref[j] = stage_ref[j].astype(BF16)

    @pl.when(m == 0)
    def _():
        @pl.when(c == 0)
        def _():
            for j in range(N_PROJ):
                copy(j, c).start()

        for j in range(N_PROJ):
            copy(j, c).wait()

        cast(order[0])

        def proj(j):
            nxt = order.index(j) + 1
            if nxt < len(order):
                cast(order[nxt])
            return dot(j)

        body(proj)

        @pl.when(c + 1 < nblk)
        def _():
            for j in range(N_PROJ):
                copy(j, c + 1).start()

    @pl.when(m > 0)
    def _():
        body(dot)


def _row_rsqrt(ssq_ref, d):
    return lax.rsqrt(jnp.sum(ssq_ref[...], axis=1, keepdims=True) * (1.0 / d) + RMS_EPS)


def _attn_inproj_kernel(xn_ref, w_hbm, wo_attn_ref, wo_conv_ref, q_ref, k_ref, v_ref, sg_ref,
                        wo_attn_bf_ref, wo_conv_bf_ref, stage_ref, wbf_ref, sem):
    heads =[slice(hh * HEAD_DIM, (hh + 1) * HEAD_DIM) for hh in range(q_ref.shape[0])]

    def body(proj):
        sg = _silu(proj(3))
        for hh, cols in enumerate(heads):
            sg_ref[hh] = sg[:, cols]
        for j, (o_ref, mult) in enumerate(((q_ref, SCORE_SCALE2), (k_ref, None), (v_ref, None))):
            p = proj(j)
            if mult is not None:
                p = p * mult
            for hh, cols in enumerate(heads):
                o_ref[hh] = p[:, cols].astype(BF16)

    _run_with_staged_weights(body, (3, 0, 1, 2), xn_ref, w_hbm, stage_ref, wbf_ref, sem)

    wo_attn_bf_ref[...] = wo_attn_ref[...].astype(BF16)
    wo_conv_bf_ref[...] = wo_conv_ref[...].astype(BF16)


def _conv_inproj_kernel(hg_ref, ssq_ref, w_hbm, cw_ref, y_ref, stage_ref, wbf_ref, sem, halo_ref):
    tm, d = hg_ref.shape
    m = pl.program_id(1)

    def body(proj):
        rs = _row_rsqrt(ssq_ref, d)
        sgr = _silu(rs * proj(3)) * rs
        cu = (rs * rs) * (proj(1) * proj(2))

        halo = jnp.where(m > 0, halo_ref[...], 0.0)
        prev1 = halo[SUBLANES - 1:SUBLANES, :]
        prev2 = halo[SUBLANES - 2:SUBLANES - 1, :]
        row = lax.broadcasted_iota(jnp.int32, cu.shape, 0)
        cu1 = jnp.where(row == 0, prev1, pltpu.roll(cu, 1, 0))
        cu2 = jnp.where(row == 0, prev2, jnp.where(row == 1, prev1, pltpu.roll(cu, 2, 0)))
        halo_ref[...] = cu[tm - SUBLANES:, :]

        cw = cw_ref[...]
        conv = cw[0:1, :] * cu2 + cw[1:2, :] * cu1 + cw[2:3, :] * cu
        y_ref[...] = (proj(0) * (conv * sgr)).astype(y_ref.dtype)

    _run_with_staged_weights(body, (3, 1, 2, 0), hg_ref, w_hbm, stage_ref, wbf_ref, sem)


def _inproj_call(kernel_fn, name, xn, w, extra_in, pre_in, pre_specs, extra_specs, out_specs, out_shape,
                 out_block_bytes, extra_scratch, tm, tn):
    s, d = xn.shape
    nblk = w.shape[1] // N_PROJ // tn
    x_spec = pl.BlockSpec((tm, d), lambda c, m: (m, 0))
    w_spec = pl.BlockSpec(memory_space=pl.ANY)
    scratch = [pltpu.VMEM((N_PROJ, d, tn), F32), pltpu.VMEM((N_PROJ, d, tn), BF16),
               pltpu.SemaphoreType.DMA((N_PROJ,))] + extra_scratch
    scratch_bytes = N_PROJ * d * tn * (4 + 2)
    temp_bytes = 8 * tm * tn * 4
    return pl.pallas_call(
        kernel_fn,
        grid=(nblk, s // tm),
        in_specs=[x_spec] + pre_specs + [w_spec] + extra_specs,
        out_specs=out_specs,
        out_shape=out_shape,
        scratch_shapes=scratch,
        compiler_params=pltpu.CompilerParams(
            dimension_semantics=("arbitrary", "arbitrary"),
            vmem_limit_bytes=_vmem_limit(tm * d * 2 + out_block_bytes, scratch_bytes, temp_bytes)),
        name=name,
    )(xn, *pre_in, w, *extra_in)


def _attn_inproj(xn, w, w_out_attn, w_out_conv, tm=1024, tn=256):
    s = xn.shape[0]
    hpt = tn // HEAD_DIM
    head_spec = pl.BlockSpec((hpt, tm, HEAD_DIM), lambda c, m: (c, m, 0))
    nm = s // tm
    kdim, n = w_out_attn.shape
    assert w_out_conv.shape == (kdim, n)
    slab = kdim // ((w.shape[1] // N_PROJ // tn) * nm)
    slab_spec = pl.BlockSpec((slab, n), lambda c, m: (c * nm + m, 0))
    return _inproj_call(
        _attn_inproj_kernel, "attn_inproj", xn, w, [w_out_attn, w_out_conv], [], [], [slab_spec] * 2,
        [head_spec] * 4 + [slab_spec] * 2,
        [jax.ShapeDtypeStruct((N_HEADS, s, HEAD_DIM), BF16)] * 3
        + [jax.ShapeDtypeStruct((N_HEADS, s, HEAD_DIM), F32)]
        + [jax.ShapeDtypeStruct((kdim, n), BF16)] * 2,
        hpt * tm * HEAD_DIM * (3 * 2 + 4) + 2 * slab * n * (4 + 2), [], tm, tn)


def _conv_inproj(hg, ssq, w, conv_w, tm=1024, tn=256):
    s = hg.shape[0]
    width = w.shape[1] // N_PROJ
    return _inproj_call(
        _conv_inproj_kernel, "conv_inproj", hg, w, [conv_w], [ssq],
        [pl.BlockSpec((tm, LANES), lambda c, m: (m, 0))],
        [pl.BlockSpec((CONV_K, tn), lambda c, m: (0, c))],
        pl.BlockSpec((tm, tn), lambda c, m: (m, c)),
        jax.ShapeDtypeStruct((s, width), BF16),
        tm * tn * 2, [pltpu.VMEM((SUBLANES, tn), F32)], tm, tn)


def _softplus2(z2):
    return jnp.maximum(z2, 0.0) + jnp.log(1.0 + jnp.exp2(-jnp.abs(z2))) * LOG2_E


def _split_hi_lo(x):
    hi = x.astype(BF16)
    lo = (x - hi.astype(F32)).astype(BF16)
    return jnp.concatenate([hi, lo], axis=1)


def _attn_kernel(q_ref, k_ref, v_ref, sg_ref, o_ref, mm_ref):
    n_heads, sq, dh = q_ref.shape
    nqb = sq // QB
    first_block = pl.program_id(1) * nqb
    nt = (((1,), (1,)), ((), ()))

    r = lax.broadcasted_iota(jnp.int32, (2 * KB, 2 * KB), 0) & (KB - 1)
    cc = lax.broadcasted_iota(jnp.int32, (2 * KB, 2 * KB), 1)
    mm_ref[...] = jnp.where((cc >= KB) | (r >= cc), -1.0, 0.0).astype(BF16)

    def scores(q, kblk):
        return lax.dot_general(q, kblk, nt, preferred_element_type=F32)

    def causal_mask():
        row = lax.broadcasted_iota(jnp.int32, (QB, KB), 0)
        col = lax.broadcasted_iota(jnp.int32, (QB, KB), 1)
        return col < row

    def windows(chains, tri, nblk, far_rows=QB):
        split_far = far_rows < QB
        near = nblk - 1 if split_far else nblk
        far0 = far_rows if split_far else 0
        wss = [pl.multiple_of((ib - (nblk - 1)) * KB, KB) for _, _, ib in chains]
        nss = [pl.multiple_of((ib - (near - 1)) * KB, KB) for _, _, ib in chains]
        zs = [scores(q_ref[g, pl.ds(r0, QB), :], k_ref[g, pl.ds(ns, near * KB), :])
              for (g, r0, _), ns in zip(chains, nss)]
        zfs = [scores(q_ref[g, pl.ds(r0, far_rows), :], k_ref[g, pl.ds(ws, KB), :])
               for (g, r0, _), ws in zip(chains, wss)] if split_far else [None] * len(chains)
        lhs = []
        for z, zf in zip(zs, zfs):
            sp = _softplus2(z)
            sps = [sp[:, b * KB:(b + 1) * KB] for b in range(near)]
            sps[-1] = jnp.where(tri, sps[-1], 0.0)
            if split_far:
                sps = [_softplus2(zf)] + sps
            lhs.append(jnp.concatenate([_split_hi_lo(s) for s in sps], axis=0))
        ps = [jnp.dot(l, mm_ref[...], preferred_element_type=F32) for l in lhs]
        a_all, a_far, carries = [], [], []
        for z, zf, p in zip(zs, zfs, ps):
            carry = None
            a_blocks = [None] * near
            for b in reversed(range(near)):
                pb = p[far0 + b * QB:far0 + (b + 1) * QB]
                log2_a = z[:, b * KB:(b + 1) * KB] + pb[:, :KB]
                if carry is not None:
                    log2_a = log2_a + carry
                a = jnp.exp2(log2_a)
                if b == near - 1:
                    a = jnp.where(tri, a, 0.0)
                a_blocks[b] = a.astype(BF16)
                carry = pb[:, KB:] if carry is None else carry + pb[:, KB:]
            if split_far:
                pf = p[:far0]
                a_far.append(jnp.exp2(zf + pf[:, :KB] + carry[:far0]).astype(BF16))
                carry = jnp.concatenate([carry[:far0] + pf[:, KB:], carry[far0:]], axis=0)
            a_all.append(jnp.concatenate(a_blocks, axis=1))
            carries.append(carry)
        accs = [jnp.dot(a, v_ref[g, pl.ds(ns, near * KB), :], preferred_element_type=F32)
                for a, (g, _, _), ns in zip(a_all, chains, nss)]
        if split_far:
            tops = [jnp.dot(a, v_ref[g, pl.ds(ws, KB), :], preferred_element_type=F32)
                    for a, (g, _, _), ws in zip(a_far, chains, wss)]
            accs = [jnp.concatenate([acc[:far0] + top, acc[far0:]], axis=0) for acc, top in zip(accs, tops)]
        return list(zip(carries, accs))

    def sweep_rest(g, r0, kb, carry, acc):
        q = q_ref[g, pl.ds(r0, QB), :]

        def cond(st):
            kb, _, _, mx = st
            return jnp.logical_and(kb >= 0, mx > LOG2_ZERO_F32)

        def body(st):
            kb, carry, acc, _ = st
            k0 = pl.multiple_of(kb * KB, KB)
            z = scores(q, k_ref[g, pl.ds(k0, KB), :])
            p = jnp.dot(_split_hi_lo(_softplus2(z)), mm_ref[...], preferred_element_type=F32)
            a = jnp.exp2(z + p[:, :KB] + carry)
            acc = acc + jnp.dot(a.astype(BF16), v_ref[g, pl.ds(k0, KB), :], preferred_element_type=F32)
            carry = carry + p[:, KB:]
            return kb - 1, carry, acc, jnp.max(carry)

        _, _, acc, _ = lax.while_loop(cond, body, (kb, carry, acc, jnp.max(carry)))
        return acc

    def qblocks(ils, nblk, far_rows=QB):
        chains = []
        for il in ils:
            r0 = il * QB if isinstance(il, int) else pl.multiple_of(il * QB, QB)
            chains += [(g, r0, first_block + il) for g in range(n_heads)]
        tri = causal_mask()
        res = windows(chains, tri, nblk, far_rows)
        live = jnp.max(functools.reduce(jnp.maximum, [c for c, _ in res])) > LOG2_ZERO_F32

        def exact():
            full = res if far_rows == QB else windows(chains, tri, nblk)
            return [sweep_rest(g, r0, ib - nblk, *rs) for (g, r0, ib), rs in zip(chains, full)]

        accs = lax.cond(live, exact, lambda: [a for _, a in res])
        for (g, r0, _), acc in zip(chains, accs):
            gated = acc * sg_ref[g, pl.ds(r0, QB), :]
            o_ref[pl.ds(r0, QB), g * dh:(g + 1) * dh] = gated.astype(o_ref.dtype)

    @pl.when(first_block == 0)
    def _():
        for il in range(WIN - 1):
            qblocks([il], il + 1)

    def body(it, _):
        qblocks([it * Q_BLOCKS_PER_ITER + j for j in range(Q_BLOCKS_PER_ITER)], WIN, FAR_ROWS)
        return 0

    assert (WIN - 1) % Q_BLOCKS_PER_ITER == 0 and nqb % Q_BLOCKS_PER_ITER == 0
    lax.fori_loop(jnp.where(first_block == 0, (WIN - 1) // Q_BLOCKS_PER_ITER, 0),
                  nqb // Q_BLOCKS_PER_ITER, body, 0)


def _attention(q, k, v, sg, heads=HEADS_PER_STEP, sq=Q_ROWS_PER_STEP):
    h, s, dh = q.shape
    q_spec = pl.BlockSpec((heads, sq, dh), lambda hg, j: (hg, j, 0))
    kv_spec = pl.BlockSpec((heads, s, dh), lambda hg, j: (hg, 0, 0))
    block_bytes = heads * dh * (sq * (2 + 4 + 2) + 2 * s * 2)
    return pl.pallas_call(
        _attn_kernel,
        grid=(h // heads, s // sq),
        in_specs=[q_spec, kv_spec, kv_spec, q_spec],
        out_specs=pl.BlockSpec((sq, heads * dh), lambda hg, j: (j, hg)),
        out_shape=jax.ShapeDtypeStruct((s, h * dh), BF16),
        scratch_shapes=[pltpu.VMEM((2 * KB, 2 * KB), BF16)],
        compiler_params=pltpu.CompilerParams(
            dimension_semantics=("arbitrary", "arbitrary"),
            vmem_limit_bytes=_vmem_limit(block_bytes, 4 * KB * KB * 2, 8 << 20)),
        name="stickbreak_attn",
    )(q, k, v, sg)


def _outproj_kernel(a_ref, w_ref, res_ref, gain_ref, h_ref, hg_ref, ssq_ref):
    half = w_ref.shape[1] // 2
    part = None
    for j in range(2):
        cols = slice(j * half, (j + 1) * half)
        h = res_ref[:, cols] + jnp.dot(a_ref[...], w_ref[:, cols], preferred_element_type=F32)
        h_ref[:, cols] = h
        hg_ref[:, cols] = (h * gain_ref[:, cols]).astype(hg_ref.dtype)
        hh = h * h
        for g in range(half // LANES):
            lanes = hh[:, g * LANES:(g + 1) * LANES]
            part = lanes if part is None else part + lanes

    @pl.when(pl.program_id(1) == 0)
    def _():
        ssq_ref[...] = part

    @pl.when(pl.program_id(1) > 0)
    def _():
        ssq_ref[...] += part


def _outproj_residual(a, w_bf16, res, gain, tm=1024, tn=1024):
    s, kdim = a.shape
    n = w_bf16.shape[1]
    block_bytes = tm * kdim * 2 + kdim * tn * 2 + tm * tn * (4 + 4 + 2) + tn * 4 + tm * LANES * 4
    tile = pl.BlockSpec((tm, tn), lambda m, j: (m, j))
    return pl.pallas_call(
        _outproj_kernel,
        grid=(s // tm, n // tn),
        in_specs=[pl.BlockSpec((tm, kdim), lambda m, j: (m, 0)),
                  pl.BlockSpec((kdim, tn), lambda m, j: (0, j)),
                  tile,
                  pl.BlockSpec((1, tn), lambda m, j: (0, j))],
        out_specs=[tile, tile, pl.BlockSpec((tm, LANES), lambda m, j: (m, 0))],
        out_shape=[jax.ShapeDtypeStruct((s, n), F32), jax.ShapeDtypeStruct((s, n), BF16),
                   jax.ShapeDtypeStruct((s, LANES), F32)],
        compiler_params=pltpu.CompilerParams(
            dimension_semantics=("arbitrary", "arbitrary"),
            vmem_limit_bytes=_vmem_limit(block_bytes, 0, 3 * tm * tn * 4)),
        name="attn_outproj",
    )(a, w_bf16, res, gain)


def _outproj_norm_kernel(a_ref, w_ref, res_ref, gain_ref, o_ref):
    n = w_ref.shape[1]
    half = n // 2
    ssq = None
    for j in range(2):
        cols = slice(j * half, (j + 1) * half)
        h = res_ref[:, cols] + jnp.dot(a_ref[...], w_ref[:, cols], preferred_element_type=F32)
        o_ref[:, cols] = h
        part = jnp.sum(h * h, axis=1, keepdims=True)
        ssq = part if ssq is None else ssq + part
    rs = lax.rsqrt(ssq * (1.0 / n) + RMS_EPS)
    o_ref[...] = o_ref[...] * rs * gain_ref[...]


def _outproj_residual_norm(a, w_bf16, res, gain, tm=256):
    s, kdim = a.shape
    n = w_bf16.shape[1]
    rows = lambda width: pl.BlockSpec((tm, width), lambda m: (m, 0))
    w_spec = pl.BlockSpec((kdim, n), lambda m: (0, 0), pipeline_mode=pl.Buffered(1))
    need = kdim * n * 2 + 2 * (tm * kdim * 2 + 2 * tm * n * 4 + n * 4) + 2 * tm * n * 4 + (2 << 20)
    return pl.pallas_call(
        _outproj_norm_kernel,
        grid=(s // tm,),
        in_specs=[rows(kdim), w_spec, rows(n), pl.BlockSpec((1, n), lambda m: (0, 0))],
        out_specs=rows(n),
        out_shape=jax.ShapeDtypeStruct((s, n), F32),
        compiler_params=pltpu.CompilerParams(
            dimension_semantics=("arbitrary",),
            vmem_limit_bytes=int(min(need, VMEM_BYTES_V7X - (4 << 20)))),
        name="conv_outproj_norm",
    )(a, w_bf16, res, gain)


def kernel(x, norm_attn, w_in_attn, w_out_attn, norm_conv, w_in_conv, conv_w, w_out_conv, final_norm):
    b, s, d = x.shape
    assert b == 1 and d == N_HEADS * HEAD_DIM
    x2 = x.reshape(s, d)
    row = lambda g: g.reshape(1, d).astype(F32)

    q, k, v, sg, wo_attn_bf, wo_conv_bf = _attn_inproj(
        _rmsnorm_bf16(x2, row(norm_attn)), w_in_attn, w_out_attn, w_out_conv)
    og = _attention(q, k, v, sg)
    h1, h1g, ssq = _outproj_residual(og, wo_attn_bf, x2, row(norm_conv))
    yg = _conv_inproj(h1g, ssq, w_in_conv, conv_w.astype(F32))
    out = _outproj_residual_norm(yg, wo_conv_bf, h1, row(final_norm))
    return out.reshape(b, s, d)
```

```python
import functools
import math

import jax
import jax.numpy as jnp
from jax import lax
from jax.experimental import pallas as pl
from jax.experimental.pallas import tpu as pltpu

F32 = jnp.float32
BF16 = jnp.bfloat16

RMS_EPS = 1e-6
N_HEADS = 32
HEAD_DIM = 128
CONV_K = 3

LANES = 128
SUBLANES = 8
VMEM_BYTES_V7X = 64 * 1024 * 1024

QB = 128
KB = 128
WIN = 3
HEADS_PER_STEP = 4
Q_ROWS_PER_STEP = 2048
Q_BLOCKS_PER_ITER = 2
FAR_ROWS = QB // 2
LOG2_E = math.log2(math.e)
SCORE_SCALE2 = LOG2_E / math.sqrt(HEAD_DIM)
LOG2_ZERO_F32 = -152.0

NORM_ROWS = 16
NORM_UNROLL = 4


def _vmem_limit(block_bytes, scratch_bytes, temp_bytes):
    need = 2 * block_bytes + scratch_bytes + temp_bytes + (4 << 20)
    return int(min(need, VMEM_BYTES_V7X - (4 << 20)))


def _silu(g):
    return g * (1.0 / (1.0 + jnp.exp(-g)))


def _rmsnorm_rows(x_ref, gain_ref, dst_ref, rows):
    group = NORM_ROWS * NORM_UNROLL

    def body(r, _):
        base = r * group
        sls = [pl.ds(pl.multiple_of(base + j * NORM_ROWS, NORM_ROWS), NORM_ROWS) for j in range(NORM_UNROLL)]
        inv = []
        for sl in sls:
            xf = x_ref[sl, :]
            inv.append(lax.rsqrt(jnp.mean(xf * xf, axis=-1, keepdims=True) + RMS_EPS))
        for sl, rs in zip(sls, inv):
            dst_ref[sl, :] = (x_ref[sl, :] * rs * gain_ref[...]).astype(dst_ref.dtype)
        return 0
    lax.fori_loop(0, rows // group, body, 0)


def _norm_kernel(x_ref, gain_ref, o_ref):
    _rmsnorm_rows(x_ref, gain_ref, o_ref, x_ref.shape[0])


def _rmsnorm_bf16(x, gain, tm=1024):
    s, d = x.shape
    return pl.pallas_call(
        _norm_kernel,
        grid=(s // tm,),
        in_specs=[pl.BlockSpec((tm, d), lambda m: (m, 0)), pl.BlockSpec((1, d), lambda m: (0, 0))],
        out_specs=pl.BlockSpec((tm, d), lambda m: (m, 0)),
        out_shape=jax.ShapeDtypeStruct((s, d), BF16),
        compiler_params=pltpu.CompilerParams(
            dimension_semantics=("arbitrary",),
            vmem_limit_bytes=_vmem_limit(tm * d * (4 + 2) + d * 4, 0, 4 << 20)),
        name="rmsnorm_bf16",
    )(x, gain)


N_PROJ = 4
CAST_ROWS = 256


def _run_with_staged_weights(body, order, xn_ref, w_hbm, stage_ref, wbf_ref, sem):
    c, nblk, m = pl.program_id(0), pl.num_programs(0), pl.program_id(1)
    tn = stage_ref.shape[2]
    width = nblk * tn

    def copy(j, cc):
        col = pl.multiple_of(j * width + cc * tn, tn)
        return pltpu.make_async_copy(w_hbm.at[:, pl.ds(col, tn)], stage_ref.at[j], sem.at[j])

    def dot(j):
        return jnp.dot(xn_ref[...], wbf_ref[j], preferred_element_type=F32)

    def cast(j):
        wbf_ref[j] = stage_ref[j].astype(BF16)

    @pl.when(m == 0)
    def _():
        @pl.when(c == 0)
        def _():
            for j in range(N_PROJ):
                copy(j, c).start()

        for j in range(N_PROJ):
            copy(j, c).wait()

        cast(order[0])

        def proj(j):
            nxt = order.index(j) + 1
            if nxt < len(order):
                cast(order[nxt])
            return dot(j)

        body(proj)

        @pl.when(c + 1 < nblk)
        def _():
            for j in range(N_PROJ):
                copy(j, c + 1).start()

    @pl.when(m > 0)
    def _():
        body(dot)


def _row_rsqrt(ssq_ref, d):
    return lax.rsqrt(jnp.sum(ssq_ref[...], axis=1, keepdims=True) * (1.0 / d) + RMS_EPS)


def _attn_inproj_kernel(xn_ref, w_hbm, wo_attn_ref, wo_conv_ref, q_ref, k_ref, v_ref, sg_ref,
                        wo_attn_bf_ref, wo_conv_bf_ref, stage_ref, wbf_ref, sem):
    heads =[slice(hh * HEAD_DIM, (hh + 1) * HEAD_DIM) for hh in range(q_ref.shape[0])]

    def body(proj):
        sg = _silu(proj(3))
        for hh, cols in enumerate(heads):
            sg_ref[hh] = sg[:, cols]
        for j, (o_ref, mult) in enumerate(((q_ref, SCORE_SCALE2), (k_ref, None), (v_ref, None))):
            p = proj(j)
            if mult is not None:
                p = p * mult
            for hh, cols in enumerate(heads):
                o_ref[hh] = p[:, cols].astype(BF16)

    _run_with_staged_weights(body, (3, 0, 1, 2), xn_ref, w_hbm, stage_ref, wbf_ref, sem)

    wo_attn_bf_ref[...] = wo_attn_ref[...].astype(BF16)
    wo_conv_bf_ref[...] = wo_conv_ref[...].astype(BF16)


def _conv_inproj_kernel(hg_ref, ssq_ref, w_hbm, cw_ref, y_ref, stage_ref, wbf_ref, sem, halo_ref):
    tm, d = hg_ref.shape
    m = pl.program_id(1)

    def body(proj):
        rs = _row_rsqrt(ssq_ref, d)
        sgr = _silu(rs * proj(3)) * rs
        cu = (rs * rs) * (proj(1) * proj(2))

        halo = jnp.where(m > 0, halo_ref[...], 0.0)
        prev1 = halo[SUBLANES - 1:SUBLANES, :]
        prev2 = halo[SUBLANES - 2:SUBLANES - 1, :]
        row = lax.broadcasted_iota(jnp.int32, cu.shape, 0)
        cu1 = jnp.where(row == 0, prev1, pltpu.roll(cu, 1, 0))
        cu2 = jnp.where(row == 0, prev2, jnp.where(row == 1, prev1, pltpu.roll(cu, 2, 0)))
        halo_ref[...] = cu[tm - SUBLANES:, :]

        cw = cw_ref[...]
        conv = cw[0:1, :] * cu2 + cw[1:2, :] * cu1 + cw[2:3, :] * cu
        y_ref[...] = (proj(0) * (conv * sgr)).astype(y_ref.dtype)

    _run_with_staged_weights(body, (3, 1, 2, 0), hg_ref, w_hbm, stage_ref, wbf_ref, sem)


def _inproj_call(kernel_fn, name, xn, w, extra_in, pre_in, pre_specs, extra_specs, out_specs, out_shape,
                 out_block_bytes, extra_scratch, tm, tn):
    s, d = xn.shape
    nblk = w.shape[1] // N_PROJ // tn
    x_spec = pl.BlockSpec((tm, d), lambda c, m: (m, 0))
    w_spec = pl.BlockSpec(memory_space=pl.ANY)
    scratch = [pltpu.VMEM((N_PROJ, d, tn), F32), pltpu.VMEM((N_PROJ, d, tn), BF16),
               pltpu.SemaphoreType.DMA((N_PROJ,))] + extra_scratch
    scratch_bytes = N_PROJ * d * tn * (4 + 2)
    temp_bytes = 8 * tm * tn * 4
    return pl.pallas_call(
        kernel_fn,
        grid=(nblk, s // tm),
        in_specs=[x_spec] + pre_specs + [w_spec] + extra_specs,
        out_specs=out_specs,
        out_shape=out_shape,
        scratch_shapes=scratch,
        compiler_params=pltpu.CompilerParams(
            dimension_semantics=("arbitrary", "arbitrary"),
            vmem_limit_bytes=_vmem_limit(tm * d * 2 + out_block_bytes, scratch_bytes, temp_bytes)),
        name=name,
    )(xn, *pre_in, w, *extra_in)


def _attn_inproj(xn, w, w_out_attn, w_out_conv, tm=1024, tn=256):
    s = xn.shape[0]
    hpt = tn // HEAD_DIM
    head_spec = pl.BlockSpec((hpt, tm, HEAD_DIM), lambda c, m: (c, m, 0))
    nm = s // tm
    kdim, n = w_out_attn.shape
    assert w_out_conv.shape == (kdim, n)
    slab = kdim // ((w.shape[1] // N_PROJ // tn) * nm)
    slab_spec = pl.BlockSpec((slab, n), lambda c, m: (c * nm + m, 0))
    return _inproj_call(
        _attn_inproj_kernel, "attn_inproj", xn, w, [w_out_attn, w_out_conv], [], [], [slab_spec] * 2,
        [head_spec] * 4 + [slab_spec] * 2,
        [jax.ShapeDtypeStruct((N_HEADS, s, HEAD_DIM), BF16)] * 3
        + [jax.ShapeDtypeStruct((N_HEADS, s, HEAD_DIM), F32)]
        + [jax.ShapeDtypeStruct((kdim, n), BF16)] * 2,
        hpt * tm * HEAD_DIM * (3 * 2 + 4) + 2 * slab * n * (4 + 2), [], tm, tn)


def _conv_inproj(hg, ssq, w, conv_w, tm=1024, tn=256):
    s = hg.shape[0]
    width = w.shape[1] // N_PROJ
    return _inproj_call(
        _conv_inproj_kernel, "conv_inproj", hg, w, [conv_w], [ssq],
        [pl.BlockSpec((tm, LANES), lambda c, m: (m, 0))],
        [pl.BlockSpec((CONV_K, tn), lambda c, m: (0, c))],
        pl.BlockSpec((tm, tn), lambda c, m: (m, c)),
        jax.ShapeDtypeStruct((s, width), BF16),
        tm * tn * 2, [pltpu.VMEM((SUBLANES, tn), F32)], tm, tn)


def _softplus2(z2):
    return jnp.maximum(z2, 0.0) + jnp.log(1.0 + jnp.exp2(-jnp.abs(z2))) * LOG2_E


def _split_hi_lo(x):
    hi = x.astype(BF16)
    lo = (x - hi.astype(F32)).astype(BF16)
    return jnp.concatenate([hi, lo], axis=1)


def _attn_kernel(q_ref, k_ref, v_ref, sg_ref, o_ref, mm_ref):
    n_heads, sq, dh = q_ref.shape
    nqb = sq // QB
    first_block = pl.program_id(1) * nqb
    nt = (((1,), (1,)), ((), ()))

    r = lax.broadcasted_iota(jnp.int32, (2 * KB, 2 * KB), 0) & (KB - 1)
    cc = lax.broadcasted_iota(jnp.int32, (2 * KB, 2 * KB), 1)
    mm_ref[...] = jnp.where((cc >= KB) | (r >= cc), -1.0, 0.0).astype(BF16)

    def scores(q, kblk):
        return lax.dot_general(q, kblk, nt, preferred_element_type=F32)

    def causal_mask():
        row = lax.broadcasted_iota(jnp.int32, (QB, KB), 0)
        col = lax.broadcasted_iota(jnp.int32, (QB, KB), 1)
        return col < row

    def windows(chains, tri, nblk, far_rows=QB):
        split_far = far_rows < QB
        near = nblk - 1 if split_far else nblk
        far0 = far_rows if split_far else 0
        wss = [pl.multiple_of((ib - (nblk - 1)) * KB, KB) for _, _, ib in chains]
        nss = [pl.multiple_of((ib - (near - 1)) * KB, KB) for _, _, ib in chains]
        zs = [scores(q_ref[g, pl.ds(r0, QB), :], k_ref[g, pl.ds(ns, near * KB), :])
              for (g, r0, _), ns in zip(chains, nss)]
        zfs = [scores(q_ref[g, pl.ds(r0, far_rows), :], k_ref[g, pl.ds(ws, KB), :])
               for (g, r0, _), ws in zip(chains, wss)] if split_far else [None] * len(chains)
        lhs = []
        for z, zf in zip(zs, zfs):
            sp = _softplus2(z)
            sps = [sp[:, b * KB:(b + 1) * KB] for b in range(near)]
            sps[-1] = jnp.where(tri, sps[-1], 0.0)
            if split_far:
                sps = [_softplus2(zf)] + sps
            lhs.append(jnp.concatenate([_split_hi_lo(s) for s in sps], axis=0))
        ps = [jnp.dot(l, mm_ref[...], preferred_element_type=F32) for l in lhs]
        a_all, a_far, carries = [], [], []
        for z, zf, p in zip(zs, zfs, ps):
            carry = None
            a_blocks = [None] * near
            for b in reversed(range(near)):
                pb = p[far0 + b * QB:far0 + (b + 1) * QB]
                log2_a = z[:, b * KB:(b + 1) * KB] + pb[:, :KB]
                if carry is not None:
                    log2_a = log2_a + carry
                a = jnp.exp2(log2_a)
                if b == near - 1:
                    a = jnp.where(tri, a, 0.0)
                a_blocks[b] = a.astype(BF16)
                carry = pb[:, KB:] if carry is None else carry + pb[:, KB:]
            if split_far:
                pf = p[:far0]
                a_far.append(jnp.exp2(zf + pf[:, :KB] + carry[:far0]).astype(BF16))
                carry = jnp.concatenate([carry[:far0] + pf[:, KB:], carry[far0:]], axis=0)
            a_all.append(jnp.concatenate(a_blocks, axis=1))
            carries.append(carry)
        accs = [jnp.dot(a, v_ref[g, pl.ds(ns, near * KB), :], preferred_element_type=F32)
                for a, (g, _, _), ns in zip(a_all, chains, nss)]
        if split_far:
            tops = [jnp.dot(a, v_ref[g, pl.ds(ws, KB), :], preferred_element_type=F32)
                    for a, (g, _, _), ws in zip(a_far, chains, wss)]
            accs = [jnp.concatenate([acc[:far0] + top, acc[far0:]], axis=0) for acc, top in zip(accs, tops)]
        return list(zip(carries, accs))

    def sweep_rest(g, r0, kb, carry, acc):
        q = q_ref[g, pl.ds(r0, QB), :]

        def cond(st):
            kb, _, _, mx = st
            return jnp.logical_and(kb >= 0, mx > LOG2_ZERO_F32)

        def body(st):
            kb, carry, acc, _ = st
            k0 = pl.multiple_of(kb * KB, KB)
            z = scores(q, k_ref[g, pl.ds(k0, KB), :])
            p = jnp.dot(_split_hi_lo(_softplus2(z)), mm_ref[...], preferred_element_type=F32)
            a = jnp.exp2(z + p[:, :KB] + carry)
            acc = acc + jnp.dot(a.astype(BF16), v_ref[g, pl.ds(k0, KB), :], preferred_element_type=F32)
            carry = carry + p[:, KB:]
            return kb - 1, carry, acc, jnp.max(carry)

        _, _, acc, _ = lax.while_loop(cond, body, (kb, carry, acc, jnp.max(carry)))
        return acc

    def qblocks(ils, nblk, far_rows=QB):
        chains = []
        for il in ils:
            r0 = il * QB if isinstance(il, int) else pl.multiple_of(il * QB, QB)
            chains += [(g, r0, first_block + il) for g in range(n_heads)]
        tri = causal_mask()
        res = windows(chains, tri, nblk, far_rows)
        live = jnp.max(functools.reduce(jnp.maximum, [c for c, _ in res])) > LOG2_ZERO_F32

        def exact():
            full = res if far_rows == QB else windows(chains, tri, nblk)
            return [sweep_rest(g, r0, ib - nblk, *rs) for (g, r0, ib), rs in zip(chains, full)]

        accs = lax.cond(live, exact, lambda: [a for _, a in res])
        for (g, r0, _), acc in zip(chains, accs):
            gated = acc * sg_ref[g, pl.ds(r0, QB), :]
            o_ref[pl.ds(r0, QB), g * dh:(g + 1) * dh] = gated.astype(o_ref.dtype)

    @pl.when(first_block == 0)
    def _():
        for il in range(WIN - 1):
            qblocks([il], il + 1)

    def body(it, _):
        qblocks([it * Q_BLOCKS_PER_ITER + j for j in range(Q_BLOCKS_PER_ITER)], WIN, FAR_ROWS)
        return 0

    assert (WIN - 1) % Q_BLOCKS_PER_ITER == 0 and nqb % Q_BLOCKS_PER_ITER == 0
    lax.fori_loop(jnp.where(first_block == 0, (WIN - 1) // Q_BLOCKS_PER_ITER, 0),
                  nqb // Q_BLOCKS_PER_ITER, body, 0)


def _attention(q, k, v, sg, heads=HEADS_PER_STEP, sq=Q_ROWS_PER_STEP):
    h, s, dh = q.shape
    q_spec = pl.BlockSpec((heads, sq, dh), lambda hg, j: (hg, j, 0))
    kv_spec = pl.BlockSpec((heads, s, dh), lambda hg, j: (hg, 0, 0))
    block_bytes = heads * dh * (sq * (2 + 4 + 2) + 2 * s * 2)
    return pl.pallas_call(
        _attn_kernel,
        grid=(h // heads, s // sq),
        in_specs=[q_spec, kv_spec, kv_spec, q_spec],
        out_specs=pl.BlockSpec((sq, heads * dh), lambda hg, j: (j, hg)),
        out_shape=jax.ShapeDtypeStruct((s, h * dh), BF16),
        scratch_shapes=[pltpu.VMEM((2 * KB, 2 * KB), BF16)],
        compiler_params=pltpu.CompilerParams(
            dimension_semantics=("arbitrary", "arbitrary"),
            vmem_limit_bytes=_vmem_limit(block_bytes, 4 * KB * KB * 2, 8 << 20)),
        name="stickbreak_attn",
    )(q, k, v, sg)


def _outproj_kernel(a_ref, w_ref, res_ref, gain_ref, h_ref, hg_ref, ssq_ref):
    half = w_ref.shape[1] // 2
    part = None
    for j in range(2):
        cols = slice(j * half, (j + 1) * half)
        h = res_ref[:, cols] + jnp.dot(a_ref[...], w_ref[:, cols], preferred_element_type=F32)
        h_ref[:, cols] = h
        hg_ref[:, cols] = (h * gain_ref[:, cols]).astype(hg_ref.dtype)
        hh = h * h
        for g in range(half // LANES):
            lanes = hh[:, g * LANES:(g + 1) * LANES]
            part = lanes if part is None else part + lanes

    @pl.when(pl.program_id(1) == 0)
    def _():
        ssq_ref[...] = part

    @pl.when(pl.program_id(1) > 0)
    def _():
        ssq_ref[...] += part


def _outproj_residual(a, w_bf16, res, gain, tm=1024, tn=1024):
    s, kdim = a.shape
    n = w_bf16.shape[1]
    block_bytes = tm * kdim * 2 + kdim * tn * 2 + tm * tn * (4 + 4 + 2) + tn * 4 + tm * LANES * 4
    tile = pl.BlockSpec((tm, tn), lambda m, j: (m, j))
    return pl.pallas_call(
        _outproj_kernel,
        grid=(s // tm, n // tn),
        in_specs=[pl.BlockSpec((tm, kdim), lambda m, j: (m, 0)),
                  pl.BlockSpec((kdim, tn), lambda m, j: (0, j)),
                  tile,
                  pl.BlockSpec((1, tn), lambda m, j: (0, j))],
        out_specs=[tile, tile, pl.BlockSpec((tm, LANES), lambda m, j: (m, 0))],
        out_shape=[jax.ShapeDtypeStruct((s, n), F32), jax.ShapeDtypeStruct((s, n), BF16),
                   jax.ShapeDtypeStruct((s, LANES), F32)],
        compiler_params=pltpu.CompilerParams(
            dimension_semantics=("arbitrary", "arbitrary"),
            vmem_limit_bytes=_vmem_limit(block_bytes, 0, 3 * tm * tn * 4)),
        name="attn_outproj",
    )(a, w_bf16, res, gain)


N_W_HALVES = 2


def _outproj_norm_kernel(a_ref, w_hbm, res_ref, gain_ref, o_ref, w_ref, sem):
    half = w_ref.shape[2]
    n = N_W_HALVES * half

    def copy(j):
        return pltpu.make_async_copy(w_hbm.at[:, pl.ds(j * half, half)], w_ref.at[j], sem.at[j])

    def body(wait):
        ssq = None
        for j in range(N_W_HALVES):
            wait(j)
            cols = slice(j * half, (j + 1) * half)
            h = res_ref[:, cols] + jnp.dot(a_ref[...], w_ref[j], preferred_element_type=F32)
            o_ref[:, cols] = h * gain_ref[:, cols]
            part = jnp.sum(h * h, axis=1, keepdims=True)
            ssq = part if ssq is None else ssq + part
        rs = lax.rsqrt(ssq * (1.0 / n) + RMS_EPS)
        o_ref[...] = o_ref[...] * rs

    @pl.when(pl.program_id(0) == 0)
    def _():
        for j in range(N_W_HALVES):
            copy(j).start()
        body(lambda j: copy(j).wait())

    @pl.when(pl.program_id(0) > 0)
    def _():
        body(lambda j: None)


def _outproj_residual_norm(a, w_bf16, res, gain, tm=256):
    s, kdim = a.shape
    n = w_bf16.shape[1]
    rows = lambda width: pl.BlockSpec((tm, width), lambda m: (m, 0))
    need = kdim * n * 2 + 2 * (tm * kdim * 2 + 2 * tm * n * 4 + n * 4) + 2 * tm * n * 4 + (2 << 20)
    return pl.pallas_call(
        _outproj_norm_kernel,
        grid=(s // tm,),
        in_specs=[rows(kdim), pl.BlockSpec(memory_space=pl.ANY), rows(n), pl.BlockSpec((1, n), lambda m: (0, 0))],
        out_specs=rows(n),
        out_shape=jax.ShapeDtypeStruct((s, n), F32),
        scratch_shapes=[pltpu.VMEM((N_W_HALVES, kdim, n // N_W_HALVES), BF16),
                        pltpu.SemaphoreType.DMA((N_W_HALVES,))],
        compiler_params=pltpu.CompilerParams(
            dimension_semantics=("arbitrary",),
            vmem_limit_bytes=int(min(need, VMEM_BYTES_V7X - (4 << 20)))),
        name="conv_outproj_norm",
    )(a, w_bf16, res, gain)


def kernel(x, norm_attn, w_in_attn, w_out_attn, norm_conv, w_in_conv, conv_w, w_out_conv, final_norm):
    b, s, d = x.shape
    assert b == 1 and d == N_HEADS * HEAD_DIM
    x2 = x.reshape(s, d)
    row = lambda g: g.reshape(1, d).astype(F32)

    q, k, v, sg, wo_attn_bf, wo_conv_bf = _attn_inproj(
        _rmsnorm_bf16(x2, row(norm_attn)), w_in_attn, w_out_attn, w_out_conv)
    og = _attention(q, k, v, sg)
    h1, h1g, ssq = _outproj_residual(og, wo_attn_bf, x2, row(norm_conv))
    yg = _conv_inproj(h1g, ssq, w_in_conv, conv_w.astype(F32))
    out = _outproj_residual_norm(yg, wo_conv_bf, h1, row(final_norm))
    return out.reshape(b, s, d)
```
